```python
import jax
import jax.numpy as jnp
from jax import lax
import numpy as np

D_MODEL = 1024
BATCH = 32
SEQ = 2048
DEPTH = 1

CTX_LEN = 256
GRID_W = 64
N_SUBLAYERS = 3
D_FF = 2816
FFN_HALF = 0.5
GLA_HEADS = 4
GLA_DK = 64
GLA_DV = 128
GLA_GATE_RANK = 16
GLA_GATE_NORMALIZER = 16.0
HGRN_HEADS = 4
HGRN_DK = 128
HGRN_DV = 128
GLA_QK = GLA_HEADS * GLA_DK
GLA_V = GLA_HEADS * GLA_DV
HGRN_K = HGRN_HEADS * HGRN_DK
HGRN_V = HGRN_HEADS * HGRN_DV
MIX_WIDTH = GLA_V + HGRN_V
IN_SPLITS = (GLA_QK, GLA_QK, GLA_V, GLA_V, GLA_GATE_RANK, GLA_GATE_RANK, HGRN_K, HGRN_K, HGRN_K, HGRN_V, HGRN_V)
IN_WIDTH = sum(IN_SPLITS)
CHUNK = 32
LN_EPS = 1e-5
NORM_EPS = 1e-6
POS_THETA = 10000.0
DN_ALPHA = (2.0 * DEPTH) ** 0.25
DN_BETA = (8.0 * DEPTH) ** -0.25

kernel_name = "hybrid_gla_hgrn2_dit_layer"


def layer_norm(x, gain=None, bias=None):
    xf = x.astype(jnp.float32)
    mu = jnp.mean(xf, axis=-1, keepdims=True)
    var = jnp.mean(jnp.square(xf - mu), axis=-1, keepdims=True)
    y = (xf - mu) * lax.rsqrt(var + LN_EPS)
    if gain is not None:
        y = y * gain.astype(jnp.float32) + bias.astype(jnp.float32)
    return y.astype(x.dtype)


def modulate(x, m, i):
    return layer_norm(x) * (1.0 + m[:, 3 * i][:, None, :]) + m[:, 3 * i + 1][:, None, :]


def residual_post_norm(x, y, gate, weight, gain, bias):
    return layer_norm(DN_ALPHA * x + weight * gate[:, None, :] * y, gain, bias)


def swiglu(h, w_in, w_out):
    gate, up = jnp.split(h @ w_in, 2, axis=-1)
    return (jax.nn.silu(gate) * up) @ w_out


def sincos_2d(rows, width, dim):
    r = jnp.repeat(jnp.arange(rows), width)
    col = jnp.tile(jnp.arange(width), rows)
    quarter = dim // 4
    omega = 1.0 / POS_THETA ** (jnp.arange(quarter, dtype=jnp.float32) / quarter)

    def emb(p):
        a = p.astype(jnp.float32)[:, None] * omega[None, :]
        return jnp.concatenate([jnp.sin(a), jnp.cos(a)], axis=-1)

    return jnp.concatenate([emb(r), emb(col)], axis=-1)


def to_heads(a, n_heads):
    b, t, _ = a.shape
    return a.reshape(b, t, n_heads, -1).transpose(0, 2, 1, 3)


def from_heads(a):
    b, h, t, d = a.shape
    return a.transpose(0, 2, 1, 3).reshape(b, t, h * d)


def chunk_gated_linear_attention(q, k, v, log_f, s0):
    bsz, nh, t, _ = q.shape
    dv = v.shape[-1]
    n_chunks = t // CHUNK

    def split(a):
        return a.astype(jnp.float32).reshape(bsz, nh, n_chunks, CHUNK, a.shape[-1]).transpose(2, 0, 1, 3, 4)

    lower_tri = jnp.tril(jnp.ones((CHUNK, CHUNK), dtype=bool))

    def step(s, blk):
        qb, kb, vb, gb = blk
        b = jnp.cumsum(gb, axis=-2)
        b_last = b[..., -1:, :]
        q_dec = qb * jnp.exp(b)
        k_inv = kb * jnp.exp(-b)
        k_tail = kb * jnp.exp(b_last - b)
        att = jnp.where(lower_tri, jnp.einsum("bhik,bhjk->bhij", q_dec, k_inv), 0.0)
        o = jnp.einsum("bhij,bhjv->bhiv", att, vb) + jnp.einsum("bhik,bhkv->bhiv", q_dec, s)
        s_new = jnp.swapaxes(jnp.exp(b_last), -1, -2) * s + jnp.einsum("bhjk,bhjv->bhkv", k_tail, vb)
        return s_new, o

    s_final, o = lax.scan(step, s0.astype(jnp.float32), (split(q), split(k), split(v), split(log_f)))
    return o.transpose(1, 2, 0, 3, 4).reshape(bsz, nh, t, dv), s_final


def bidirectional_scan(lat, ctx):
    q, k_f, k_b, v, g_f, g_b = lat
    qc, kc_f, kc_b, vc, gc_f, gc_b = ctx

    def flip(a):
        return jnp.flip(a, axis=2)

    zero = jnp.zeros(qc.shape[:2] + (qc.shape[-1], vc.shape[-1]), jnp.float32)
    oc_f, sc_f = chunk_gated_linear_attention(qc, kc_f, vc, gc_f, zero)
    oc_b, sc_b = chunk_gated_linear_attention(flip(qc), flip(kc_b), flip(vc), flip(gc_b), zero)
    o_f, _ = chunk_gated_linear_attention(q, k_f, v, g_f, sc_f)
    o_b, _ = chunk_gated_linear_attention(flip(q), flip(k_b), flip(v), flip(g_b), sc_b)
    return o_f + flip(o_b), oc_f + flip(oc_b)


def mixer_features(h, w_in, a2_f, a2_b, a_bias_f, a_bias_b, lb_f, lb_b):
    split_points = np.cumsum(IN_SPLITS)[:-1].tolist()
    (g_q, g_k, g_v, g_gate, g_lr_f, g_lr_b, r_q, r_f_f, r_f_b, r_i, r_gate) = jnp.split(h @ w_in, split_points, axis=-1)

    def gla_log_decay(lr, a2, bias):
        return to_heads(jax.nn.log_sigmoid((lr @ a2 + bias).astype(jnp.float32)) / GLA_GATE_NORMALIZER, GLA_HEADS)

    def hgrn_forget(z, lb):
        f = lb + (1.0 - lb) * jax.nn.sigmoid(z.astype(jnp.float32))
        return to_heads(jnp.log(f), HGRN_HEADS), to_heads(1.0 - f, HGRN_HEADS)

    k_gla = to_heads(g_k, GLA_HEADS)
    gla = (to_heads(g_q * GLA_DK ** -0.5, GLA_HEADS), k_gla, k_gla, to_heads(g_v, GLA_HEADS),
           gla_log_decay(g_lr_f, a2_f, a_bias_f), gla_log_decay(g_lr_b, a2_b, a_bias_b))
    logf_f, k_f = hgrn_forget(r_f_f, lb_f)
    logf_b, k_b = hgrn_forget(r_f_b, lb_b)
    hgrn = (to_heads(jax.nn.silu(r_q) * HGRN_DK ** -0.5, HGRN_HEADS), k_f, k_b, to_heads(r_i, HGRN_HEADS), logf_f, logf_b)
    return gla, hgrn, g_gate, r_gate


def gated_head_norm(o, gain, gate):
    o = o * lax.rsqrt(jnp.mean(jnp.square(o), axis=-1, keepdims=True) + NORM_EPS) * gain.astype(jnp.float32)
    return from_heads(o) * jax.nn.silu(gate.astype(jnp.float32))


def token_mixer(h, hc, with_ctx_output, w_in, a2_f, a2_b, a_bias_f, a_bias_b, lb_f, lb_b, gla_gain, hgrn_gain, w_out):
    gla, hgrn, gla_gate, hgrn_gate = mixer_features(h, w_in, a2_f, a2_b, a_bias_f, a_bias_b, lb_f, lb_b)
    gla_c, hgrn_c, gla_gate_c, hgrn_gate_c = mixer_features(hc, w_in, a2_f, a2_b, a_bias_f, a_bias_b, lb_f, lb_b)
    o_gla, oc_gla = bidirectional_scan(gla, gla_c)
    o_hgrn, oc_hgrn = bidirectional_scan(hgrn, hgrn_c)

    def project(o1, g1, o2, g2):
        merged = jnp.concatenate([gated_head_norm(o1, gla_gain, g1), gated_head_norm(o2, hgrn_gain, g2)], axis=-1)
        return merged.astype(w_out.dtype) @ w_out

    y = project(o_gla, gla_gate, o_hgrn, hgrn_gate)
    yc = project(oc_gla, gla_gate_c, oc_hgrn, hgrn_gate_c) if with_ctx_output else None
    return y, yc


def trunk_layer(x, xc, m, mc, update_ctx, ln_gain, ln_bias, ffn1_w_in, ffn1_w_out, w_mix_in,
                a2_f, a2_b, a_bias_f, a_bias_b, lb_f, lb_b, gla_gain, hgrn_gain, w_mix_out, ffn2_w_in, ffn2_w_out):
    x = residual_post_norm(x, swiglu(modulate(x, m, 0), ffn1_w_in, ffn1_w_out), m[:, 2], FFN_HALF, ln_gain[0], ln_bias[0])
    xc = residual_post_norm(xc, swiglu(modulate(xc, mc, 0), ffn1_w_in, ffn1_w_out), mc[:, 2], FFN_HALF, ln_gain[0], ln_bias[0])
    y, yc = token_mixer(modulate(x, m, 1), modulate(xc, mc, 1), update_ctx, w_mix_in, a2_f, a2_b, a_bias_f, a_bias_b,
                        lb_f, lb_b, gla_gain, hgrn_gain, w_mix_out)
    x = residual_post_norm(x, y, m[:, 5], 1.0, ln_gain[1], ln_bias[1])
    x = residual_post_norm(x, swiglu(modulate(x, m, 2), ffn2_w_in, ffn2_w_out), m[:, 8], FFN_HALF, ln_gain[2], ln_bias[2])
    if update_ctx:
        xc = residual_post_norm(xc, yc, mc[:, 5], 1.0, ln_gain[1], ln_bias[1])
        xc = residual_post_norm(xc, swiglu(modulate(xc, mc, 2), ffn2_w_in, ffn2_w_out), mc[:, 8], FFN_HALF, ln_gain[2], ln_bias[2])
    return x, xc


def setup_inputs(seed: int = 0) -> dict:
    key = jax.random.key(seed)
    ks = jax.random.split(key, 24)
    f32 = jnp.float32

    def w(k, shape, fan_in, scale=1.0):
        return jax.random.normal(k, shape, f32) * (scale * fan_in ** -0.5)

    def noise(k, shape, scale):
        return jax.random.normal(k, shape, f32) * scale

    n_mod = 3 * N_SUBLAYERS
    return {
        "x": jax.random.normal(ks[0], (BATCH, SEQ, D_MODEL), f32),
        "c": jax.random.normal(ks[1], (BATCH, D_MODEL), f32),
        "ctx": jax.random.normal(ks[2], (BATCH, CTX_LEN, D_MODEL), f32),
        "c_ctx": jax.random.normal(ks[3], (D_MODEL,), f32),
        "w_ada": w(ks[4], (DEPTH, D_MODEL, n_mod * D_MODEL), D_MODEL, 0.5),
        "b_ada": noise(ks[5], (DEPTH, n_mod * D_MODEL), 0.02),
        "ln_gain": 1.0 + noise(ks[6], (DEPTH, N_SUBLAYERS, D_MODEL), 0.05),
        "ln_bias": noise(ks[7], (DEPTH, N_SUBLAYERS, D_MODEL), 0.02),
        "ffn1_w_in": w(ks[8], (DEPTH, D_MODEL, 2 * D_FF), D_MODEL),
        "ffn1_w_out": w(ks[9], (DEPTH, D_FF, D_MODEL), D_FF, DN_BETA),
        "w_mix_in": w(ks[10], (DEPTH, D_MODEL, IN_WIDTH), D_MODEL),
        "gla_a2_fwd": w(ks[11], (DEPTH, GLA_GATE_RANK, GLA_QK), GLA_GATE_RANK),
        "gla_a2_bwd": w(ks[12], (DEPTH, GLA_GATE_RANK, GLA_QK), GLA_GATE_RANK),
        "gla_a_bias_fwd": noise(ks[13], (DEPTH, GLA_QK), 0.1),
        "gla_a_bias_bwd": noise(ks[14], (DEPTH, GLA_QK), 0.1),
        "hgrn_lb_logits": noise(ks[15], (2, DEPTH + 1, HGRN_K), 0.1),
        "gla_norm_gain": 1.0 + noise(ks[16], (DEPTH, GLA_DV), 0.05),
        "hgrn_norm_gain": 1.0 + noise(ks[17], (DEPTH, HGRN_DV), 0.05),
        "w_mix_out": w(ks[18], (DEPTH, MIX_WIDTH, D_MODEL), MIX_WIDTH, DN_BETA),
        "ffn2_w_in": w(ks[19], (DEPTH, D_MODEL, 2 * D_FF), D_MODEL),
        "ffn2_w_out": w(ks[20], (DEPTH, D_FF, D_MODEL), D_FF, DN_BETA),
    }


def reference(x, c, ctx, c_ctx, w_ada, b_ada, ln_gain, ln_bias, ffn1_w_in, ffn1_w_out, w_mix_in,
              gla_a2_fwd, gla_a2_bwd, gla_a_bias_fwd, gla_a_bias_bwd, hgrn_lb_logits,
              gla_norm_gain, hgrn_norm_gain, w_mix_out, ffn2_w_in, ffn2_w_out):
    bsz, n_tok, d = x.shape
    rows = n_tok // GRID_W
    x = x + sincos_2d(rows, GRID_W, d).astype(x.dtype)[None]
    xc = ctx
    lb = jnp.cumsum(jax.nn.softmax(hgrn_lb_logits.astype(jnp.float32), axis=1), axis=1)
    n_mod = 3 * N_SUBLAYERS
    for l in range(DEPTH):
        m = (jax.nn.silu(c) @ w_ada[l] + b_ada[l]).reshape(bsz, n_mod, d)
        mc = (jax.nn.silu(c_ctx) @ w_ada[l] + b_ada[l]).reshape(1, n_mod, d)
        x, xc = trunk_layer(x, xc, m, mc, l < DEPTH - 1, ln_gain[l], ln_bias[l], ffn1_w_in[l], ffn1_w_out[l], w_mix_in[l],
                            gla_a2_fwd[l], gla_a2_bwd[l], gla_a_bias_fwd[l], gla_a_bias_bwd[l], lb[0, l], lb[1, l],
                            gla_norm_gain[l], hgrn_norm_gain[l], w_mix_out[l], ffn2_w_in[l], ffn2_w_out[l])
    return x
```

```python
import functools

import numpy as np
import jax
import jax.numpy as jnp
from jax import lax
from jax.experimental import pallas as pl
from jax.experimental.pallas import tpu as pltpu

N_SUBLAYERS = 3
N_MOD = 3 * N_SUBLAYERS
FFN_HALF = 0.5
GLA_HEADS = 4
GLA_DK = 64
GLA_DV = 128
GLA_GATE_RANK = 16
GLA_GATE_NORMALIZER = 16.0
HGRN_HEADS = 4
HGRN_DK = 128
HGRN_DV = 128
GLA_QK = GLA_HEADS * GLA_DK
GLA_V = GLA_HEADS * GLA_DV
HGRN_K = HGRN_HEADS * HGRN_DK
HGRN_V = HGRN_HEADS * HGRN_DV
CHUNK = 32
LN_EPS = 1e-5
NORM_EPS = 1e-6
POS_THETA = 10000.0
GRID_W = 64
DEPTH = 1
DN_ALPHA = (2.0 * DEPTH) ** 0.25

LANES = 128
FF_CHUNK = 256
DECAY_W = GLA_QK + HGRN_K
N_GROUPS = GLA_HEADS // 2 + HGRN_HEADS
VMEM_LIMIT = 56 * 1024 * 1024

_C_GQ, _C_GK, _C_GV, _C_GG = 0, 256, 512, 1024
_C_RQ, _C_RFF, _C_RFB, _C_RI, _C_RG, _C_LR = 1536, 2048, 2560, 3072, 3584, 4096
MIX_W = 4096 + LANES

F32 = jnp.float32
BF16 = jnp.bfloat16


def _dot(a, b):
    return jnp.dot(a, b, preferred_element_type=F32)


def _dot_nt(a, b):
    return lax.dot_general(a, b, (((1,), (1,)), ((), ())), preferred_element_type=F32)


def _dot_tn(a, b):
    return lax.dot_general(a, b, (((0,), (0,)), ((), ())), preferred_element_type=F32)


def _sigmoid(x):
    return 1.0 / (1.0 + jnp.exp(-x))


def _silu(x):
    return x * _sigmoid(x)


def _ln(x):
    mu = jnp.mean(x, axis=-1, keepdims=True)
    xc = x - mu
    var = jnp.mean(xc * xc, axis=-1, keepdims=True)
    return xc * lax.rsqrt(var + LN_EPS)


def _swiglu(h, w_in_ref, w_out_ref, a_scr):
    n = w_in_ref.shape[0]
    for j in range(n):
        gu = _dot(h, w_in_ref[j])
        g = gu[:, :FF_CHUNK]
        u = gu[:, FF_CHUNK:]
        a_scr[:, j * FF_CHUNK:(j + 1) * FF_CHUNK] = (_silu(g) * u).astype(BF16)
    return _dot(a_scr[...], w_out_ref[...])


def _post_norm(x, y, gate, weight, gain, bias):
    return _ln(DN_ALPHA * x + (weight * gate) * y) * gain + bias


def _ada_kernel(c_ref, w_ref, b_ref, o_ref):
    s = _silu(c_ref[...]).astype(BF16)
    o_ref[...] = _dot(s, w_ref[...].astype(BF16)) + b_ref[...]


def _ada_call(cc, w, b):
    rows, d = cc.shape
    n = w.shape[1]
    tn = 1536
    return pl.pallas_call(
        _ada_kernel,
        grid=(n // tn,),
        in_specs=[pl.BlockSpec((rows, d), lambda j: (0, 0)),
                  pl.BlockSpec((d, tn), lambda j: (0, j)),
                  pl.BlockSpec((1, tn), lambda j: (0, j))],
        out_specs=pl.BlockSpec((rows, tn), lambda j: (0, j)),
        out_shape=jax.ShapeDtypeStruct((rows, n), F32),
        compiler_params=pltpu.CompilerParams(dimension_semantics=("arbitrary",), vmem_limit_bytes=VMEM_LIMIT),
        name="ada",
    )(cc, w, b)


def _seg_cumsum(x, reverse):
    row = lax.broadcasted_iota(jnp.int32, (CHUNK, 1), 0)
    s = 1
    while s < CHUNK:
        if reverse:
            x = x + jnp.where(row < CHUNK - s, pltpu.roll(x, CHUNK - s, 0), 0.0)
        else:
            x = x + jnp.where(row >= s, pltpu.roll(x, s, 0), 0.0)
        s *= 2
    return x


def _ffn_proj_kernel(latent, *refs):
    if latent:
        (x_ref, pos_ref, mod_ref, lng_ref, lnb_ref, w1i_ref, w1o_ref, wmix_ref, a2_ref, ab_ref, lbl_ref,
         x1_ref, ff_ref, fb_ref, v_ref, g_ref, dlf_ref, dlb_ref, a_scr) = refs
    else:
        (x_ref, mod_ref, lng_ref, lnb_ref, w1i_ref, w1o_ref, wmix_ref, a2_ref, ab_ref, lbl_ref,
         ff_ref, fb_ref, v_ref, dlf_ref, dlb_ref, a_scr) = refs
    tm = x_ref.shape[1]
    m = mod_ref[0]
    x = x_ref[0]
    if latent:
        x = x + pos_ref[...]
    h = (_ln(x) * (1.0 + m[0:1]) + m[1:2]).astype(BF16)
    y = _swiglu(h, w1i_ref, w1o_ref, a_scr)
    x1 = _post_norm(x, y, m[2:3], FFN_HALF, lng_ref[0:1], lnb_ref[0:1])
    if latent:
        x1_ref[0] = x1
    h2 = (_ln(x1) * (1.0 + m[3:4]) + m[4:5]).astype(BF16)

    def proj(c0, width):
        return _dot(h2, wmix_ref[:, c0:c0 + width])

    lr = proj(_C_LR, LANES).astype(BF16)
    z = _dot(lr, a2_ref[...]) + ab_ref[...]
    g_gla = (jnp.minimum(z, 0.0) - jnp.log(1.0 + jnp.exp(-jnp.abs(z)))) * (1.0 / GLA_GATE_NORMALIZER)
    k_gla = proj(_C_GK, GLA_QK)
    v_ref[0, :, 0:GLA_V] = proj(_C_GV, GLA_V).astype(BF16)
    v_ref[0, :, GLA_V:GLA_V + HGRN_V] = proj(_C_RI, HGRN_V).astype(BF16)
    if latent:
        q_gla = proj(_C_GQ, GLA_QK) * (GLA_DK ** -0.5)
        q_h = _silu(proj(_C_RQ, HGRN_K)) * (HGRN_DK ** -0.5)
        g_ref[0, :, 0:GLA_V] = proj(_C_GG, GLA_V).astype(BF16)
        g_ref[0, :, GLA_V:GLA_V + HGRN_V] = proj(_C_RG, HGRN_V).astype(BF16)

    lbl = lbl_ref[...]
    n_lev = lbl.shape[1]

    for dirn, (c_rf, f_ref, dl_ref) in enumerate(((_C_RFF, ff_ref, dlf_ref), (_C_RFB, fb_ref, dlb_ref))):
        rows = [lbl[dirn, i:i + 1, :] for i in range(n_lev)]
        mx = functools.reduce(jnp.maximum, rows)
        es = [jnp.exp(r - mx) for r in rows]
        lb = es[0] / functools.reduce(lambda a, b: a + b, es)
        f = lb + (1.0 - lb) * _sigmoid(proj(c_rf, HGRN_K))
        logf = jnp.log(f)
        kk = 1.0 - f
        g_d = g_gla[:, dirn * GLA_QK:(dirn + 1) * GLA_QK]
        for c in range(tm // CHUNK):
            r0, r1 = c * CHUNK, (c + 1) * CHUNK
            b = _seg_cumsum(jnp.concatenate([g_d[r0:r1], logf[r0:r1]], axis=1), reverse=(dirn == 1))
            bl = b[0:1] if dirn == 1 else b[CHUNK - 1:CHUNK]
            kcat = jnp.concatenate([k_gla[r0:r1], kk[r0:r1]], axis=1)
            if latent:
                qcat = jnp.concatenate([q_gla[r0:r1], q_h[r0:r1]], axis=1)
                f_ref[0, r0:r1, 0:DECAY_W] = (qcat * jnp.exp(b)).astype(BF16)
                f_ref[0, r0:r1, DECAY_W:2 * DECAY_W] = (kcat * jnp.exp(-b)).astype(BF16)
                f_ref[0, r0:r1, 2 * DECAY_W:3 * DECAY_W] = (kcat * jnp.exp(bl - b)).astype(BF16)
            else:
                f_ref[0, r0:r1, :] = (kcat * jnp.exp(bl - b)).astype(BF16)
            dl_ref[0, c:c + 1, :] = jnp.exp(bl)


def _const_spec(shape):
    nd = len(shape)
    return pl.BlockSpec(shape, lambda *_: (0,) * nd, pipeline_mode=pl.Buffered(1))


def _ffn_proj_call(latent, x, pos, mod, mod_row0, lng, lnb, w1i, w1o, wmix, a2, ab, lbl, tm):
    bsz, t, d = x.shape
    nt = t // tm
    d_ff = w1o.shape[0]
    cpt = tm // CHUNK
    tok = lambda w: pl.BlockSpec((1, tm, w), lambda b, i: (b, i, 0))
    in_specs = [tok(d)]
    args = [x]
    if latent:
        in_specs.append(pl.BlockSpec((tm, d), lambda b, i: (i, 0)))
        args.append(pos)
        mod_map = lambda b, i: (b, 0, 0)
    else:
        mod_map = lambda b, i: (mod_row0, 0, 0)
    in_specs += [pl.BlockSpec((1, N_MOD, d), mod_map),
                 _const_spec(lng.shape), _const_spec(lnb.shape), _const_spec(w1i.shape), _const_spec(w1o.shape),
                 _const_spec(wmix.shape), _const_spec(a2.shape), _const_spec(ab.shape), _const_spec(lbl.shape)]
    args += [mod, lng, lnb, w1i, w1o, wmix, a2, ab, lbl]
    fw = 3 * DECAY_W if latent else DECAY_W
    dl_spec = pl.BlockSpec((1, cpt, DECAY_W), lambda b, i: (b, i, 0))
    dl_shape = jax.ShapeDtypeStruct((bsz, t // CHUNK, DECAY_W), F32)
    f_shape = jax.ShapeDtypeStruct((bsz, t, fw), BF16)
    v_shape = jax.ShapeDtypeStruct((bsz, t, GLA_V + HGRN_V), BF16)
    if latent:
        out_specs = [tok(d), tok(fw), tok(fw), tok(GLA_V + HGRN_V), tok(GLA_V + HGRN_V), dl_spec, dl_spec]
        out_shape = [jax.ShapeDtypeStruct((bsz, t, d), F32), f_shape, f_shape, v_shape, v_shape, dl_shape, dl_shape]
    else:
        out_specs = [tok(fw), tok(fw), tok(GLA_V + HGRN_V), dl_spec, dl_spec]
        out_shape = [f_shape, f_shape, v_shape, dl_shape, dl_shape]
    return pl.pallas_call(
        functools.partial(_ffn_proj_kernel, latent),
        grid=(bsz, nt),
        in_specs=in_specs,
        out_specs=out_specs,
        out_shape=out_shape,
        scratch_shapes=[pltpu.VMEM((tm, d_ff), BF16)],
        compiler_params=pltpu.CompilerParams(dimension_semantics=("arbitrary", "arbitrary"),
                                             vmem_limit_bytes=VMEM_LIMIT),
        name="ffn_proj_latent" if latent else "ffn_proj_ctx",
    )(*args)


def _lane_masks():
    lane = lax.broadcasted_iota(jnp.int32, (1, LANES), 1)
    return lane < GLA_DK, lane >= GLA_DK


def _stack_pair(a, m0, m1):
    zero = jnp.zeros_like(a)
    return jnp.concatenate([jnp.where(m0, a, zero), jnp.where(m1, a, zero)], axis=0)


def _state_update(st_ref, idx, dl, grp, kt, v):
    m0, m1 = _lane_masks()
    ktg = kt[:, grp * LANES:(grp + 1) * LANES]
    dlg = dl[:, grp * LANES:(grp + 1) * LANES]
    if grp < GLA_HEADS // 2:
        h0 = 2 * grp
        vs = jnp.concatenate([v[:, h0 * GLA_DV:(h0 + 1) * GLA_DV], v[:, (h0 + 1) * GLA_DV:(h0 + 2) * GLA_DV]], axis=0)
        ks = _stack_pair(ktg, m0, m1)
    else:
        h = grp - GLA_HEADS // 2
        vs = v[:, GLA_V + h * HGRN_DV:GLA_V + (h + 1) * HGRN_DV]
        ks = ktg
    st_ref[idx] = dlg * st_ref[idx] + _dot_tn(vs, ks)


def _ctx_state_kernel(ktf_ref, ktb_ref, v_ref, dlf_ref, dlb_ref, sf_ref, sb_ref):
    n_chunks = v_ref.shape[1] // CHUNK
    sf_ref[...] = jnp.zeros_like(sf_ref)
    sb_ref[...] = jnp.zeros_like(sb_ref)
    for i in range(n_chunks):
        for kt_ref, dl_ref, s_ref, c in ((ktf_ref, dlf_ref, sf_ref, i), (ktb_ref, dlb_ref, sb_ref, n_chunks - 1 - i)):
            kt = kt_ref[0, c * CHUNK:(c + 1) * CHUNK, :]
            v = v_ref[0, c * CHUNK:(c + 1) * CHUNK, :]
            dl = dl_ref[0, c:c + 1, :]
            for grp in range(N_GROUPS):
                _state_update(s_ref.at[0], grp, dl, grp, kt, v)


def _ctx_state_call(ktf, ktb, v, dlf, dlb):
    bsz, tc, _ = v.shape
    full = lambda a: pl.BlockSpec((1,) + a.shape[1:], lambda b: (b,) + (0,) * (a.ndim - 1))
    s_shape = jax.ShapeDtypeStruct((bsz, N_GROUPS, LANES, LANES), F32)
    s_spec = pl.BlockSpec((1, N_GROUPS, LANES, LANES), lambda b: (b, 0, 0, 0))
    return pl.pallas_call(
        _ctx_state_kernel,
        grid=(bsz,),
        in_specs=[full(ktf), full(ktb), full(v), full(dlf), full(dlb)],
        out_specs=[s_spec, s_spec],
        out_shape=[s_shape, s_shape],
        compiler_params=pltpu.CompilerParams(dimension_semantics=("arbitrary",), vmem_limit_bytes=VMEM_LIMIT),
        name="ctx_state",
    )(ktf, ktb, v, dlf, dlb)


def _scan_chunk(reverse, f_ref, v_ref, dl_ref, o_ref, st_ref, c):
    r0 = pl.multiple_of(c * CHUNK, CHUNK)
    rows = pl.ds(r0, CHUNK)
    ii = lax.broadcasted_iota(jnp.int32, (CHUNK, CHUNK), 0)
    jj = lax.broadcasted_iota(jnp.int32, (CHUNK, CHUNK), 1)
    tri = (jj >= ii) if reverse else (jj <= ii)
    m0, m1 = _lane_masks()
    v = v_ref[0, rows, :]
    dl = dl_ref[0, pl.ds(c, 1), :]
    kt = f_ref[0, rows, 2 * DECAY_W:3 * DECAY_W]
    for grp in range(N_GROUPS):
        lo = grp * LANES
        qd = f_ref[0, rows, lo:lo + LANES]
        ki = f_ref[0, rows, DECAY_W + lo:DECAY_W + lo + LANES]
        st = st_ref[grp].astype(BF16)
        if grp < GLA_HEADS // 2:
            qs = _stack_pair(qd, m0, m1)
            att = _dot_nt(qs, ki)
            inter = _dot_nt(qs, st)
            for hh in range(2):
                h = 2 * grp + hh
                a = jnp.where(tri, att[hh * CHUNK:(hh + 1) * CHUNK], 0.0).astype(BF16)
                o_ref[0, rows, h * GLA_DV:(h + 1) * GLA_DV] = (
                    _dot(a, v[:, h * GLA_DV:(h + 1) * GLA_DV]) + inter[hh * CHUNK:(hh + 1) * CHUNK])
        else:
            h = grp - GLA_HEADS // 2
            c0 = GLA_V + h * HGRN_DV
            a = jnp.where(tri, _dot_nt(qd, ki), 0.0).astype(BF16)
            o_ref[0, rows, c0:c0 + HGRN_DV] = _dot(a, v[:, c0:c0 + HGRN_DV]) + _dot_nt(qd, st)
        _state_update(st_ref, grp, dl, grp, kt, v)


def _scan_kernel(ff_ref, fb_ref, vf_ref, vb_ref, dlf_ref, dlb_ref, sf0_ref, sb0_ref, of_ref, ob_ref, sf_scr, sb_scr):
    n_chunks = vf_ref.shape[1] // CHUNK

    @pl.when(pl.program_id(1) == 0)
    def _():
        sf_scr[...] = sf0_ref[0]
        sb_scr[...] = sb0_ref[0]

    def body(i, carry):
        _scan_chunk(False, ff_ref, vf_ref, dlf_ref, of_ref, sf_scr, i)
        _scan_chunk(True, fb_ref, vb_ref, dlb_ref, ob_ref, sb_scr, n_chunks - 1 - i)
        return carry

    lax.fori_loop(0, n_chunks, body, 0)


def _scan_call(ff, fb, v, dlf, dlb, sf0, sb0, tb):
    bsz, t, _ = v.shape
    nt = t // tb
    cpt = tb // CHUNK
    fwd = lambda w: pl.BlockSpec((1, tb, w), lambda b, i: (b, i, 0))
    bwd = lambda w: pl.BlockSpec((1, tb, w), lambda b, i: (b, nt - 1 - i, 0))
    s_spec = pl.BlockSpec((1, N_GROUPS, LANES, LANES), lambda b, i: (b, 0, 0, 0))
    o_shape = jax.ShapeDtypeStruct((bsz, t, GLA_V + HGRN_V), F32)
    return pl.pallas_call(
        _scan_kernel,
        grid=(bsz, nt),
        in_specs=[fwd(ff.shape[2]), bwd(fb.shape[2]), fwd(v.shape[2]), bwd(v.shape[2]),
                  pl.BlockSpec((1, cpt, DECAY_W), lambda b, i: (b, i, 0)),
                  pl.BlockSpec((1, cpt, DECAY_W), lambda b, i: (b, nt - 1 - i, 0)),
                  s_spec, s_spec],
        out_specs=[fwd(GLA_V + HGRN_V), bwd(GLA_V + HGRN_V)],
        out_shape=[o_shape, o_shape],
        scratch_shapes=[pltpu.VMEM((N_GROUPS, LANES, LANES), F32), pltpu.VMEM((N_GROUPS, LANES, LANES), F32)],
        compiler_params=pltpu.CompilerParams(dimension_semantics=("arbitrary", "arbitrary"),
                                             vmem_limit_bytes=VMEM_LIMIT),
        name="scan",
    )(ff, fb, v, v, dlf, dlb, sf0, sb0)


def _out_ffn_kernel(x1_ref, of_ref, ob_ref, g_ref, mod_ref, lng_ref, lnb_ref, hg_ref, wo_ref, w2i_ref, w2o_ref,
                    out_ref, a_scr, mg_scr):
    m = mod_ref[0]
    x1 = x1_ref[0]
    o = of_ref[0] + ob_ref[0]
    n_heads = o.shape[1] // LANES
    for h in range(n_heads):
        oh = o[:, h * LANES:(h + 1) * LANES]
        ms = jnp.mean(oh * oh, axis=-1, keepdims=True)
        gate = g_ref[0, :, h * LANES:(h + 1) * LANES].astype(F32)
        mg_scr[:, h * LANES:(h + 1) * LANES] = (
            oh * lax.rsqrt(ms + NORM_EPS) * hg_ref[:, h * LANES:(h + 1) * LANES] * _silu(gate)).astype(BF16)
    y = _dot(mg_scr[...], wo_ref[...])
    x2 = _post_norm(x1, y, m[5:6], 1.0, lng_ref[1:2], lnb_ref[1:2])
    h3 = (_ln(x2) * (1.0 + m[6:7]) + m[7:8]).astype(BF16)
    y2 = _swiglu(h3, w2i_ref, w2o_ref, a_scr)
    out_ref[0] = _post_norm(x2, y2, m[8:9], FFN_HALF, lng_ref[2:3], lnb_ref[2:3])


def _out_ffn_call(x1, o_f, o_b, g, mod, lng, lnb, hgain, wo, w2i, w2o, tm):
    bsz, t, d = x1.shape
    nt = t // tm
    d_ff = w2o.shape[0]
    tok = lambda w: pl.BlockSpec((1, tm, w), lambda b, i: (b, i, 0))
    return pl.pallas_call(
        _out_ffn_kernel,
        grid=(bsz, nt),
        in_specs=[tok(d), tok(o_f.shape[2]), tok(o_b.shape[2]), tok(g.shape[2]),
                  pl.BlockSpec((1, N_MOD, d), lambda b, i: (b, 0, 0)),
                  _const_spec(lng.shape), _const_spec(lnb.shape), _const_spec(hgain.shape),
                  _const_spec(wo.shape), _const_spec(w2i.shape), _const_spec(w2o.shape)],
        out_specs=tok(d),
        out_shape=jax.ShapeDtypeStruct((bsz, t, d), F32),
        scratch_shapes=[pltpu.VMEM((tm, d_ff), BF16), pltpu.VMEM((tm, o_f.shape[2]), BF16)],
        compiler_params=pltpu.CompilerParams(dimension_semantics=("arbitrary", "arbitrary"),
                                             vmem_limit_bytes=VMEM_LIMIT),
        name="out_ffn",
    )(x1, o_f, o_b, g, mod, lng, lnb, hgain, wo, w2i, w2o)


def _sincos_2d(rows, width, dim):
    r = jnp.repeat(jnp.arange(rows), width)
    col = jnp.tile(jnp.arange(width), rows)
    quarter = dim // 4
    omega = 1.0 / POS_THETA ** (jnp.arange(quarter, dtype=F32) / quarter)

    def emb(p):
        a = p.astype(F32)[:, None] * omega[None, :]
        return jnp.concatenate([jnp.sin(a), jnp.cos(a)], axis=-1)

    return jnp.concatenate([emb(r), emb(col)], axis=-1)


def _ffn_weights(w_in, w_out):
    d, two_ff = w_in.shape
    d_ff = two_ff // 2
    n = d_ff // FF_CHUNK
    wg = w_in[:, :d_ff].reshape(d, n, FF_CHUNK)
    wu = w_in[:, d_ff:].reshape(d, n, FF_CHUNK)
    w = jnp.concatenate([wg, wu], axis=2).transpose(1, 0, 2)
    return w.astype(BF16), w_out.astype(BF16)


def _mix_weights(w_mix_in, a2_f, a2_b, ab_f, ab_b):
    d = w_mix_in.shape[0]
    o = np.cumsum((0, GLA_QK, GLA_QK, GLA_V, GLA_V, GLA_GATE_RANK, GLA_GATE_RANK, HGRN_K, HGRN_K, HGRN_K, HGRN_V, HGRN_V))
    seg = lambda i: w_mix_in[:, o[i]:o[i + 1]]
    pad = jnp.zeros((d, LANES - 2 * GLA_GATE_RANK), w_mix_in.dtype)
    wmix = jnp.concatenate([seg(0), seg(1), seg(2), seg(3), seg(6), seg(7), seg(8), seg(9), seg(10), seg(4), seg(5), pad],
                           axis=1).astype(BF16)
    a2 = jnp.zeros((LANES, 2 * GLA_QK), F32)
    a2 = a2.at[0:GLA_GATE_RANK, 0:GLA_QK].set(a2_f).at[GLA_GATE_RANK:2 * GLA_GATE_RANK, GLA_QK:].set(a2_b)
    ab = jnp.concatenate([ab_f, ab_b])[None, :]
    return wmix, a2.astype(BF16), ab


def kernel(x, c, ctx, c_ctx, w_ada, b_ada, ln_gain, ln_bias, ffn1_w_in, ffn1_w_out, w_mix_in, gla_a2_fwd, gla_a2_bwd,
           gla_a_bias_fwd, gla_a_bias_bwd, hgrn_lb_logits, gla_norm_gain, hgrn_norm_gain, w_mix_out, ffn2_w_in,
           ffn2_w_out):
    bsz, t, d = x.shape
    tm = min(256, t)
    tmc = min(256, ctx.shape[1])
    pos = _sincos_2d(t // GRID_W, GRID_W, d).astype(x.dtype)

    mod_rows = -(-(bsz + 1) // 8) * 8
    cc = jnp.zeros((mod_rows, d), F32).at[:bsz].set(c).at[bsz].set(c_ctx)
    mod = _ada_call(cc, w_ada[0], b_ada[0][None, :]).reshape(mod_rows, N_MOD, d)

    w1i, w1o = _ffn_weights(ffn1_w_in[0], ffn1_w_out[0])
    w2i, w2o = _ffn_weights(ffn2_w_in[0], ffn2_w_out[0])
    wmix, a2, ab = _mix_weights(w_mix_in[0], gla_a2_fwd[0], gla_a2_bwd[0], gla_a_bias_fwd[0], gla_a_bias_bwd[0])
    lng, lnb = ln_gain[0], ln_bias[0]
    hgain = jnp.concatenate([jnp.tile(gla_norm_gain[0], GLA_HEADS), jnp.tile(hgrn_norm_gain[0], HGRN_HEADS)])[None, :]

    ktf_c, ktb_c, v_c, dlf_c, dlb_c = _ffn_proj_call(False, ctx, None, mod, bsz, lng, lnb, w1i, w1o, wmix, a2, ab,
                                                     hgrn_lb_logits, tmc)
    sf0, sb0 = _ctx_state_call(ktf_c, ktb_c, v_c, dlf_c, dlb_c)
    x1, ff, fb, v, g, dlf, dlb = _ffn_proj_call(True, x, pos, mod, bsz, lng, lnb, w1i, w1o, wmix, a2, ab,
                                                hgrn_lb_logits, tm)
    o_f, o_b = _scan_call(ff, fb, v, dlf, dlb, sf0, sb0, tm)
    return _out_ffn_call(x1, o_f, o_b, g, mod, lng, lnb, hgain, w_mix_out[0].astype(BF16), w2i, w2o, tm)
```

```python
import functools

import numpy as np
import jax
import jax.numpy as jnp
from jax import lax
from jax.experimental import pallas as pl
from jax.experimental.pallas import tpu as pltpu

N_SUBLAYERS = 3
N_MOD = 3 * N_SUBLAYERS
FFN_HALF = 0.5
GLA_HEADS = 4
GLA_DK = 64
GLA_DV = 128
GLA_GATE_RANK = 16
GLA_GATE_NORMALIZER = 16.0
HGRN_HEADS = 4
HGRN_DK = 128
HGRN_DV = 128
GLA_QK = GLA_HEADS * GLA_DK
GLA_V = GLA_HEADS * GLA_DV
HGRN_K = HGRN_HEADS * HGRN_DK
HGRN_V = HGRN_HEADS * HGRN_DV
CHUNK = 32
CHUNK_SHIFT = 5
LN_EPS = 1e-5
NORM_EPS = 1e-6
POS_THETA = 10000.0
GRID_W = 64
DEPTH = 1
DN_ALPHA = (2.0 * DEPTH) ** 0.25

LANES = 128
FF_CHUNK = 256
DECAY_W = GLA_QK + HGRN_K
N_GROUPS = GLA_HEADS // 2 + HGRN_HEADS
N_HEADS = GLA_HEADS + HGRN_HEADS
VMEM_LIMIT = 56 * 1024 * 1024

_C_GQ, _C_GK, _C_GV, _C_GG = 0, 256, 512, 1024
_C_RQ, _C_RFF, _C_RFB, _C_RI, _C_RG, _C_LR = 1536, 2048, 2560, 3072, 3584, 4096
MIX_W = 4096 + LANES

F32 = jnp.float32
BF16 = jnp.bfloat16


def _dot(a, b):
    return jnp.dot(a, b, preferred_element_type=F32)


def _dot_nt(a, b):
    return lax.dot_general(a, b, (((1,), (1,)), ((), ())), preferred_element_type=F32)


def _dot_tn(a, b):
    return lax.dot_general(a, b, (((0,), (0,)), ((), ())), preferred_element_type=F32)


def _sigmoid(x):
    return 1.0 / (1.0 + jnp.exp(-x))


def _silu(x):
    return x * _sigmoid(x)


def _ln(x):
    mu = jnp.mean(x, axis=-1, keepdims=True)
    xc = x - mu
    var = jnp.mean(xc * xc, axis=-1, keepdims=True)
    return xc * lax.rsqrt(var + LN_EPS)


def _swiglu(h, w_in_ref, w_out_ref, a_scr):
    n = w_in_ref.shape[0]
    for j in range(n):
        gu = _dot(h, w_in_ref[j])
        g = gu[:, :FF_CHUNK]
        u = gu[:, FF_CHUNK:]
        a_scr[:, j * FF_CHUNK:(j + 1) * FF_CHUNK] = (_silu(g) * u).astype(BF16)
    return _dot(a_scr[...], w_out_ref[...])


def _post_norm(x, y, gate, weight, gain, bias):
    return _ln(DN_ALPHA * x + (weight * gate) * y) * gain + bias


def _ada_kernel(c_ref, w_ref, b_ref, o_ref):
    s = _silu(c_ref[...]).astype(BF16)
    o_ref[...] = _dot(s, w_ref[...].astype(BF16)) + b_ref[...]


def _ada_call(cc, w, b):
    rows, d = cc.shape
    n = w.shape[1]
    tn = 1536
    return pl.pallas_call(
        _ada_kernel,
        grid=(n // tn,),
        in_specs=[pl.BlockSpec((rows, d), lambda j: (0, 0)),
                  pl.BlockSpec((d, tn), lambda j: (0, j)),
                  pl.BlockSpec((1, tn), lambda j: (0, j))],
        out_specs=pl.BlockSpec((rows, tn), lambda j: (0, j)),
        out_shape=jax.ShapeDtypeStruct((rows, n), F32),
        compiler_params=pltpu.CompilerParams(dimension_semantics=("arbitrary",), vmem_limit_bytes=VMEM_LIMIT),
        name="ada",
    )(cc, w, b)


def _lane_masks():
    lane = lax.broadcasted_iota(jnp.int32, (1, LANES), 1)
    return lane < GLA_DK, lane >= GLA_DK


def _stack_pair(a, m0, m1):
    zero = jnp.zeros_like(a)
    return jnp.concatenate([jnp.where(m0, a, zero), jnp.where(m1, a, zero)], axis=0)


def _state_update(st_ref, dl, grp, kt, v):
    m0, m1 = _lane_masks()
    ktg = kt[:, grp * LANES:(grp + 1) * LANES]
    dlg = dl[:, grp * LANES:(grp + 1) * LANES]
    if grp < GLA_HEADS // 2:
        h0 = 2 * grp
        vs = jnp.concatenate([v[:, h0 * GLA_DV:(h0 + 1) * GLA_DV], v[:, (h0 + 1) * GLA_DV:(h0 + 2) * GLA_DV]], axis=0)
        ks = _stack_pair(ktg, m0, m1)
    else:
        h = grp - GLA_HEADS // 2
        vs = v[:, GLA_V + h * HGRN_DV:GLA_V + (h + 1) * HGRN_DV]
        ks = ktg
    st_ref[grp] = dlg * st_ref[grp] + _dot_tn(vs, ks)


def _seg_cumsum(x, reverse):
    row = lax.broadcasted_iota(jnp.int32, (CHUNK, 1), 0)
    s = 1
    while s < CHUNK:
        if reverse:
            x = x + jnp.where(row < CHUNK - s, pltpu.roll(x, CHUNK - s, 0), 0.0)
        else:
            x = x + jnp.where(row >= s, pltpu.roll(x, s, 0), 0.0)
        s *= 2
    return x


def _ffn_proj_kernel(latent, *refs):
    if latent:
        (x_ref, pos_ref, mod_ref, lng_ref, lnb_ref, w1i_ref, w1o_ref, wmix_ref, a2_ref, ab_ref, lbl_ref, sb0_ref,
         x1_ref, ff_ref, fb_ref, v_ref, g_ref, dlf_ref, dlb_ref, sbo_ref, a_scr, kb_scr, sb_scr) = refs
    else:
        (x_ref, mod_ref, lng_ref, lnb_ref, w1i_ref, w1o_ref, wmix_ref, a2_ref, ab_ref, lbl_ref,
         ff_ref, fb_ref, v_ref, dlf_ref, dlb_ref, a_scr) = refs
    tm = x_ref.shape[1]
    m = mod_ref[0]
    x = x_ref[0]
    if latent:
        x = x + pos_ref[...]
    h = (_ln(x) * (1.0 + m[0:1]) + m[1:2]).astype(BF16)
    y = _swiglu(h, w1i_ref, w1o_ref, a_scr)
    x1 = _post_norm(x, y, m[2:3], FFN_HALF, lng_ref[0:1], lnb_ref[0:1])
    if latent:
        x1_ref[0] = x1
    h2 = (_ln(x1) * (1.0 + m[3:4]) + m[4:5]).astype(BF16)

    def proj(c0, width):
        return _dot(h2, wmix_ref[:, c0:c0 + width])

    lr = proj(_C_LR, LANES).astype(BF16)
    z = _dot(lr, a2_ref[...]) + ab_ref[...]
    g_gla = (jnp.minimum(z, 0.0) - jnp.log(1.0 + jnp.exp(-jnp.abs(z)))) * (1.0 / GLA_GATE_NORMALIZER)
    k_gla = proj(_C_GK, GLA_QK)
    v_ref[0, :, 0:GLA_V] = proj(_C_GV, GLA_V).astype(BF16)
    v_ref[0, :, GLA_V:GLA_V + HGRN_V] = proj(_C_RI, HGRN_V).astype(BF16)
    if latent:
        q_gla = proj(_C_GQ, GLA_QK) * (GLA_DK ** -0.5)
        q_h = _silu(proj(_C_RQ, HGRN_K)) * (HGRN_DK ** -0.5)
        g_ref[0, :, 0:GLA_V] = proj(_C_GG, GLA_V).astype(BF16)
        g_ref[0, :, GLA_V:GLA_V + HGRN_V] = proj(_C_RG, HGRN_V).astype(BF16)

    lbl = lbl_ref[...]
    n_lev = lbl.shape[1]

    for dirn, (c_rf, f_ref, dl_ref) in enumerate(((_C_RFF, ff_ref, dlf_ref), (_C_RFB, fb_ref, dlb_ref))):
        rows = [lbl[dirn, i:i + 1, :] for i in range(n_lev)]
        mx = functools.reduce(jnp.maximum, rows)
        es = [jnp.exp(r - mx) for r in rows]
        lb = es[0] / functools.reduce(lambda a, b: a + b, es)
        f = lb + (1.0 - lb) * _sigmoid(proj(c_rf, HGRN_K))
        logf = jnp.log(f)
        kk = 1.0 - f
        g_d = g_gla[:, dirn * GLA_QK:(dirn + 1) * GLA_QK]
        run = None
        for c in range(tm // CHUNK):
            r0, r1 = c * CHUNK, (c + 1) * CHUNK
            b = _seg_cumsum(jnp.concatenate([g_d[r0:r1], logf[r0:r1]], axis=1), reverse=(dirn == 1))
            bl = b[0:1] if dirn == 1 else b[CHUNK - 1:CHUNK]
            kcat = jnp.concatenate([k_gla[r0:r1], kk[r0:r1]], axis=1)
            ktail = kcat * jnp.exp(bl - b)
            dl = jnp.exp(bl)
            if latent:
                qcat = jnp.concatenate([q_gla[r0:r1], q_h[r0:r1]], axis=1)
                f_ref[0, r0:r1, 0:DECAY_W] = (qcat * jnp.exp(b)).astype(BF16)
                f_ref[0, r0:r1, DECAY_W:2 * DECAY_W] = (kcat * jnp.exp(-b)).astype(BF16)
                f_ref[0, r0:r1, 2 * DECAY_W:3 * DECAY_W] = ktail.astype(BF16)
                if dirn == 1:
                    kb_scr[r0:r1, :] = (ktail if run is None else ktail * run).astype(BF16)
                    run = dl if run is None else run * dl
            else:
                f_ref[0, r0:r1, :] = ktail.astype(BF16)
            dl_ref[0, c:c + 1, :] = dl

    if latent:
        @pl.when(pl.program_id(1) == 0)
        def _():
            sb_scr[...] = sb0_ref[0]

        sbo_ref[0, 0] = sb_scr[...]
        kb = kb_scr[...]
        vv = v_ref[0]
        for grp in range(N_GROUPS):
            _state_update(sb_scr, run, grp, kb, vv)


def _const_spec(shape):
    nd = len(shape)
    return pl.BlockSpec(shape, lambda *_: (0,) * nd, pipeline_mode=pl.Buffered(1))


def _ffn_proj_call(latent, x, pos, mod, mod_row0, lng, lnb, w1i, w1o, wmix, a2, ab, lbl, sb0, tm):
    bsz, t, d = x.shape
    nt = t // tm
    d_ff = w1o.shape[0]
    cpt = tm // CHUNK
    tile = (lambda b, i: (b, nt - 1 - i, 0)) if latent else (lambda b, i: (b, i, 0))
    tok = lambda w: pl.BlockSpec((1, tm, w), tile)
    in_specs = [tok(d)]
    args = [x]
    if latent:
        in_specs.append(pl.BlockSpec((tm, d), lambda b, i: (nt - 1 - i, 0)))
        args.append(pos)
        mod_map = lambda b, i: (b, 0, 0)
    else:
        mod_map = lambda b, i: (mod_row0, 0, 0)
    in_specs += [pl.BlockSpec((1, N_MOD, d), mod_map),
                 _const_spec(lng.shape), _const_spec(lnb.shape), _const_spec(w1i.shape), _const_spec(w1o.shape),
                 _const_spec(wmix.shape), _const_spec(a2.shape), _const_spec(ab.shape), _const_spec(lbl.shape)]
    args += [mod, lng, lnb, w1i, w1o, wmix, a2, ab, lbl]
    fw = 3 * DECAY_W if latent else DECAY_W
    dl_spec = pl.BlockSpec((1, cpt, DECAY_W), tile)
    dl_shape = jax.ShapeDtypeStruct((bsz, t // CHUNK, DECAY_W), F32)
    f_shape = jax.ShapeDtypeStruct((bsz, t, fw), BF16)
    v_shape = jax.ShapeDtypeStruct((bsz, t, GLA_V + HGRN_V), BF16)
    scratch = [pltpu.VMEM((tm, d_ff), BF16)]
    if latent:
        s_block = (1, N_GROUPS, LANES, LANES)
        in_specs.append(pl.BlockSpec(s_block, lambda b, i: (b, 0, 0, 0)))
        args.append(sb0)
        out_specs = [tok(d), tok(fw), tok(fw), tok(GLA_V + HGRN_V), tok(GLA_V + HGRN_V), dl_spec, dl_spec,
                     pl.BlockSpec((1,) + s_block, lambda b, i: (b, nt - 1 - i, 0, 0, 0))]
        out_shape = [jax.ShapeDtypeStruct((bsz, t, d), F32), f_shape, f_shape, v_shape, v_shape, dl_shape, dl_shape,
                     jax.ShapeDtypeStruct((bsz, nt) + s_block[1:], F32)]
        scratch += [pltpu.VMEM((tm, DECAY_W), BF16), pltpu.VMEM(s_block[1:], F32)]
    else:
        out_specs = [tok(fw), tok(fw), tok(GLA_V + HGRN_V), dl_spec, dl_spec]
        out_shape = [f_shape, f_shape, v_shape, dl_shape, dl_shape]
    return pl.pallas_call(
        functools.partial(_ffn_proj_kernel, latent),
        grid=(bsz, nt),
        in_specs=in_specs,
        out_specs=out_specs,
        out_shape=out_shape,
        scratch_shapes=scratch,
        compiler_params=pltpu.CompilerParams(dimension_semantics=("arbitrary", "arbitrary"),
                                             vmem_limit_bytes=VMEM_LIMIT),
        name="ffn_proj_latent" if latent else "ffn_proj_ctx",
    )(*args)


def _ctx_state_kernel(ktf_ref, ktb_ref, v_ref, dlf_ref, dlb_ref, sf_ref, sb_ref):
    n_chunks = v_ref.shape[1] // CHUNK
    sf_ref[...] = jnp.zeros_like(sf_ref)
    sb_ref[...] = jnp.zeros_like(sb_ref)
    for i in range(n_chunks):
        for kt_ref, dl_ref, s_ref, c in ((ktf_ref, dlf_ref, sf_ref, i), (ktb_ref, dlb_ref, sb_ref, n_chunks - 1 - i)):
            kt = kt_ref[0, c * CHUNK:(c + 1) * CHUNK, :]
            v = v_ref[0, c * CHUNK:(c + 1) * CHUNK, :]
            dl = dl_ref[0, c:c + 1, :]
            for grp in range(N_GROUPS):
                _state_update(s_ref.at[0], dl, grp, kt, v)


def _ctx_state_call(ktf, ktb, v, dlf, dlb):
    bsz, tc, _ = v.shape
    full = lambda a: pl.BlockSpec((1,) + a.shape[1:], lambda b: (b,) + (0,) * (a.ndim - 1))
    s_shape = jax.ShapeDtypeStruct((bsz, N_GROUPS, LANES, LANES), F32)
    s_spec = pl.BlockSpec((1, N_GROUPS, LANES, LANES), lambda b: (b, 0, 0, 0))
    return pl.pallas_call(
        _ctx_state_kernel,
        grid=(bsz,),
        in_specs=[full(ktf), full(ktb), full(v), full(dlf), full(dlb)],
        out_specs=[s_spec, s_spec],
        out_shape=[s_shape, s_shape],
        compiler_params=pltpu.CompilerParams(dimension_semantics=("arbitrary",), vmem_limit_bytes=VMEM_LIMIT),
        name="ctx_state",
    )(ktf, ktb, v, dlf, dlb)


def _scan_kernel(ff_ref, fb_ref, v_ref, dlf_ref, dlb_ref, sb_ref, sf0_ref, o_ref, sf_scr):
    tb = v_ref.shape[1]
    nc = tb // CHUNK

    @pl.when(pl.program_id(1) == 0)
    def _():
        for h in range(N_HEADS):
            sf_scr[h] = sf0_ref[0, h // 2 if h < GLA_HEADS else h - GLA_HEADS // 2]

    ii = lax.broadcasted_iota(jnp.int32, (tb, tb), 0)
    jj = lax.broadcasted_iota(jnp.int32, (tb, tb), 1)
    same = (ii >> CHUNK_SHIFT) == (jj >> CHUNK_SHIFT)
    mask_f = same & (jj <= ii)
    mask_b = same & (jj >= ii)
    tok_chunk = lax.broadcasted_iota(jnp.int32, (1, tb), 1) >> CHUNK_SHIFT
    m_pair = _lane_masks()

    for h in range(N_HEADS):
        grp = h // 2 if h < GLA_HEADS else h - GLA_HEADS // 2
        lo = grp * LANES
        vcol = h * LANES
        qf = ff_ref[0, :, lo:lo + LANES]
        qb = fb_ref[0, :, lo:lo + LANES]
        if h < GLA_HEADS:
            qf = jnp.where(m_pair[h % 2], qf, jnp.zeros_like(qf))
            qb = jnp.where(m_pair[h % 2], qb, jnp.zeros_like(qb))
        kif = ff_ref[0, :, DECAY_W + lo:DECAY_W + lo + LANES]
        kib = fb_ref[0, :, DECAY_W + lo:DECAY_W + lo + LANES]
        ktf = ff_ref[0, :, 2 * DECAY_W + lo:2 * DECAY_W + lo + LANES]
        ktb = fb_ref[0, :, 2 * DECAY_W + lo:2 * DECAY_W + lo + LANES]
        vh = v_ref[0, :, vcol:vcol + LANES]

        vt = vh.T
        zero = jnp.zeros_like(vt)
        lhs = jnp.concatenate([jnp.where(tok_chunk == c, vt, zero) for c in range(nc)], axis=0)
        u = _dot(lhs, jnp.concatenate([ktf, ktb], axis=1))

        st_f = [None] * nc
        cur = sf_scr[h]
        for c in range(nc):
            st_f[c] = cur.astype(BF16)
            cur = dlf_ref[0, c:c + 1, lo:lo + LANES] * cur + u[c * LANES:(c + 1) * LANES, 0:LANES]
        sf_scr[h] = cur
        st_b = [None] * nc
        cur = sb_ref[0, 0, grp]
        for c in reversed(range(nc)):
            st_b[c] = cur.astype(BF16)
            cur = dlb_ref[0, c:c + 1, lo:lo + LANES] * cur + u[c * LANES:(c + 1) * LANES, LANES:2 * LANES]

        att = jnp.where(mask_f, _dot_nt(qf, kif), 0.0) + jnp.where(mask_b, _dot_nt(qb, kib), 0.0)
        o = _dot(att.astype(BF16), vh)
        for c in range(nc):
            r0, r1 = c * CHUNK, (c + 1) * CHUNK
            qc = jnp.concatenate([qf[r0:r1], qb[r0:r1]], axis=1)
            sc = jnp.concatenate([st_f[c], st_b[c]], axis=1)
            o_ref[0, r0:r1, vcol:vcol + LANES] = o[r0:r1] + _dot_nt(qc, sc)


def _scan_call(ff, fb, v, dlf, dlb, sb, sf0, tb):
    bsz, t, _ = v.shape
    nt = t // tb
    cpt = tb // CHUNK
    tok = lambda w: pl.BlockSpec((1, tb, w), lambda b, i: (b, i, 0))
    s_block = (1, N_GROUPS, LANES, LANES)
    return pl.pallas_call(
        _scan_kernel,
        grid=(bsz, nt),
        in_specs=[tok(ff.shape[2]), tok(fb.shape[2]), tok(v.shape[2]),
                  pl.BlockSpec((1, cpt, DECAY_W), lambda b, i: (b, i, 0)),
                  pl.BlockSpec((1, cpt, DECAY_W), lambda b, i: (b, i, 0)),
                  pl.BlockSpec((1,) + s_block, lambda b, i: (b, i, 0, 0, 0)),
                  pl.BlockSpec(s_block, lambda b, i: (b, 0, 0, 0))],
        out_specs=tok(GLA_V + HGRN_V),
        out_shape=jax.ShapeDtypeStruct((bsz, t, GLA_V + HGRN_V), F32),
        scratch_shapes=[pltpu.VMEM((N_HEADS, LANES, LANES), F32)],
        compiler_params=pltpu.CompilerParams(dimension_semantics=("arbitrary", "arbitrary"),
                                             vmem_limit_bytes=VMEM_LIMIT),
        name="scan",
    )(ff, fb, v, dlf, dlb, sb, sf0)


def _out_ffn_kernel(x1_ref, o_ref, g_ref, mod_ref, lng_ref, lnb_ref, hg_ref, wo_ref, w2i_ref, w2o_ref,
                    out_ref, a_scr, mg_scr):
    m = mod_ref[0]
    x1 = x1_ref[0]
    for h in range(N_HEADS):
        oh = o_ref[0, :, h * LANES:(h + 1) * LANES]
        ms = jnp.mean(oh * oh, axis=-1, keepdims=True)
        gate = g_ref[0, :, h * LANES:(h + 1) * LANES].astype(F32)
        mg_scr[:, h * LANES:(h + 1) * LANES] = (
            oh * lax.rsqrt(ms + NORM_EPS) * hg_ref[:, h * LANES:(h + 1) * LANES] * _silu(gate)).astype(BF16)
    y = _dot(mg_scr[...], wo_ref[...])
    x2 = _post_norm(x1, y, m[5:6], 1.0, lng_ref[1:2], lnb_ref[1:2])
    h3 = (_ln(x2) * (1.0 + m[6:7]) + m[7:8]).astype(BF16)
    y2 = _swiglu(h3, w2i_ref, w2o_ref, a_scr)
    out_ref[0] = _post_norm(x2, y2, m[8:9], FFN_HALF, lng_ref[2:3], lnb_ref[2:3])


def _out_ffn_call(x1, o, g, mod, lng, lnb, hgain, wo, w2i, w2o, tm):
    bsz, t, d = x1.shape
    nt = t // tm
    d_ff = w2o.shape[0]
    tok = lambda w: pl.BlockSpec((1, tm, w), lambda b, i: (b, i, 0))
    return pl.pallas_call(
        _out_ffn_kernel,
        grid=(bsz, nt),
        in_specs=[tok(d), tok(o.shape[2]), tok(g.shape[2]),
                  pl.BlockSpec((1, N_MOD, d), lambda b, i: (b, 0, 0)),
                  _const_spec(lng.shape), _const_spec(lnb.shape), _const_spec(hgain.shape),
                  _const_spec(wo.shape), _const_spec(w2i.shape), _const_spec(w2o.shape)],
        out_specs=tok(d),
        out_shape=jax.ShapeDtypeStruct((bsz, t, d), F32),
        scratch_shapes=[pltpu.VMEM((tm, d_ff), BF16), pltpu.VMEM((tm, o.shape[2]), BF16)],
        compiler_params=pltpu.CompilerParams(dimension_semantics=("arbitrary", "arbitrary"),
                                             vmem_limit_bytes=VMEM_LIMIT),
        name="out_ffn",
    )(x1, o, g, mod, lng, lnb, hgain, wo, w2i, w2o)


def _sincos_2d(rows, width, dim):
    r = jnp.repeat(jnp.arange(rows), width)
    col = jnp.tile(jnp.arange(width), rows)
    quarter = dim // 4
    omega = 1.0 / POS_THETA ** (jnp.arange(quarter, dtype=F32) / quarter)

    def emb(p):
        a = p.astype(F32)[:, None] * omega[None, :]
        return jnp.concatenate([jnp.sin(a), jnp.cos(a)], axis=-1)

    return jnp.concatenate([emb(r), emb(col)], axis=-1)


def _ffn_weights(w_in, w_out):
    d, two_ff = w_in.shape
    d_ff = two_ff // 2
    n = d_ff // FF_CHUNK
    wg = w_in[:, :d_ff].reshape(d, n, FF_CHUNK)
    wu = w_in[:, d_ff:].reshape(d, n, FF_CHUNK)
    w = jnp.concatenate([wg, wu], axis=2).transpose(1, 0, 2)
    return w.astype(BF16), w_out.astype(BF16)


def _mix_weights(w_mix_in, a2_f, a2_b, ab_f, ab_b):
    d = w_mix_in.shape[0]
    o = np.cumsum((0, GLA_QK, GLA_QK, GLA_V, GLA_V, GLA_GATE_RANK, GLA_GATE_RANK, HGRN_K, HGRN_K, HGRN_K, HGRN_V, HGRN_V))
    seg = lambda i: w_mix_in[:, o[i]:o[i + 1]]
    pad = jnp.zeros((d, LANES - 2 * GLA_GATE_RANK), w_mix_in.dtype)
    wmix = jnp.concatenate([seg(0), seg(1), seg(2), seg(3), seg(6), seg(7), seg(8), seg(9), seg(10), seg(4), seg(5), pad],
                           axis=1).astype(BF16)
    a2 = jnp.zeros((LANES, 2 * GLA_QK), F32)
    a2 = a2.at[0:GLA_GATE_RANK, 0:GLA_QK].set(a2_f).at[GLA_GATE_RANK:2 * GLA_GATE_RANK, GLA_QK:].set(a2_b)
    ab = jnp.concatenate([ab_f, ab_b])[None, :]
    return wmix, a2.astype(BF16), ab


def kernel(x, c, ctx, c_ctx, w_ada, b_ada, ln_gain, ln_bias, ffn1_w_in, ffn1_w_out, w_mix_in, gla_a2_fwd, gla_a2_bwd,
           gla_a_bias_fwd, gla_a_bias_bwd, hgrn_lb_logits, gla_norm_gain, hgrn_norm_gain, w_mix_out, ffn2_w_in,
           ffn2_w_out):
    bsz, t, d = x.shape
    tm = min(256, t)
    tmc = min(256, ctx.shape[1])
    pos = _sincos_2d(t // GRID_W, GRID_W, d).astype(x.dtype)

    mod_rows = -(-(bsz + 1) // 8) * 8
    cc = jnp.zeros((mod_rows, d), F32).at[:bsz].set(c).at[bsz].set(c_ctx)
    mod = _ada_call(cc, w_ada[0], b_ada[0][None, :]).reshape(mod_rows, N_MOD, d)

    w1i, w1o = _ffn_weights(ffn1_w_in[0], ffn1_w_out[0])
    w2i, w2o = _ffn_weights(ffn2_w_in[0], ffn2_w_out[0])
    wmix, a2, ab = _mix_weights(w_mix_in[0], gla_a2_fwd[0], gla_a2_bwd[0], gla_a_bias_fwd[0], gla_a_bias_bwd[0])
    lng, lnb = ln_gain[0], ln_bias[0]
    hgain = jnp.concatenate([jnp.tile(gla_norm_gain[0], GLA_HEADS), jnp.tile(hgrn_norm_gain[0], HGRN_HEADS)])[None, :]

    ktf_c, ktb_c, v_c, dlf_c, dlb_c = _ffn_proj_call(False, ctx, None, mod, bsz, lng, lnb, w1i, w1o, wmix, a2, ab,
                                                     hgrn_lb_logits, None, tmc)
    sf0, sb0 = _ctx_state_call(ktf_c, ktb_c, v_c, dlf_c, dlb_c)
    x1, ff, fb, v, g, dlf, dlb, sb = _ffn_proj_call(True, x, pos, mod, bsz, lng, lnb, w1i, w1o, wmix, a2, ab,
                                                    hgrn_lb_logits, sb0, tm)
    o = _scan_call(ff, fb, v, dlf, dlb, sb, sf0, tm)
    return _out_ffn_call(x1, o, g, mod, lng, lnb, hgain, w_mix_out[0].astype(BF16), w2i, w2o, tm)
```

```python
import functools

import numpy as np
import jax
import jax.numpy as jnp
from jax import lax
from jax.experimental import pallas as pl
from jax.experimental.pallas import tpu as pltpu

N_SUBLAYERS = 3
N_MOD = 3 * N_SUBLAYERS
FFN_HALF = 0.5
GLA_HEADS = 4
GLA_DK = 64
GLA_DV = 128
GLA_GATE_RANK = 16
GLA_GATE_NORMALIZER = 16.0
HGRN_HEADS = 4
HGRN_DK = 128
HGRN_DV = 128
GLA_QK = GLA_HEADS * GLA_DK
GLA_V = GLA_HEADS * GLA_DV
HGRN_K = HGRN_HEADS * HGRN_DK
HGRN_V = HGRN_HEADS * HGRN_DV
CHUNK = 32
CHUNK_SHIFT = 5
LN_EPS = 1e-5
NORM_EPS = 1e-6
POS_THETA = 10000.0
GRID_W = 64
DEPTH = 1
DN_ALPHA = (2.0 * DEPTH) ** 0.25

LANES = 128
SUBLANES = 8
FF_CHUNK = 256
SUB_TILE = 256
DECAY_W = GLA_QK + HGRN_K
N_GROUPS = GLA_HEADS // 2 + HGRN_HEADS
N_HEADS = GLA_HEADS + HGRN_HEADS
VMEM_LIMIT = 56 * 1024 * 1024

_C_GQ, _C_GK, _C_GV, _C_GG = 0, 256, 512, 1024
_C_RQ, _C_RFF, _C_RFB, _C_RI, _C_RG, _C_LR = 1536, 2048, 2560, 3072, 3584, 4096
MIX_W = 4096 + LANES

F32 = jnp.float32
BF16 = jnp.bfloat16


def _dot(a, b):
    return jnp.dot(a, b, preferred_element_type=F32)


def _dot_nt(a, b):
    return lax.dot_general(a, b, (((1,), (1,)), ((), ())), preferred_element_type=F32)


def _dot_tn(a, b):
    return lax.dot_general(a, b, (((0,), (0,)), ((), ())), preferred_element_type=F32)


def _sigmoid(x):
    return 1.0 / (1.0 + jnp.exp(-x))


def _silu(x):
    return x * _sigmoid(x)


def _ln(x):
    mu = jnp.mean(x, axis=-1, keepdims=True)
    xc = x - mu
    var = jnp.mean(xc * xc, axis=-1, keepdims=True)
    return xc * lax.rsqrt(var + LN_EPS)


def _swiglu(h, w_in_ref, w_out_ref, a_scr):
    d_ff = w_out_ref.shape[0]
    for c0 in range(0, d_ff, FF_CHUNK):
        g = _dot(h, w_in_ref[:, c0:c0 + FF_CHUNK])
        u = _dot(h, w_in_ref[:, d_ff + c0:d_ff + c0 + FF_CHUNK])
        a_scr[:, c0:c0 + FF_CHUNK] = (_silu(g) * u).astype(BF16)
    return _dot(a_scr[...], w_out_ref[...])


def _post_norm(x, y, gate, weight, gain, bias):
    return _ln(DN_ALPHA * x + (weight * gate) * y) * gain + bias


def _ada_kernel(c_ref, w_ref, b_ref, o_ref):
    s = _silu(c_ref[...]).astype(BF16)
    o_ref[...] = _dot(s, w_ref[...].astype(BF16)) + b_ref[...]


def _ada_call(cc, w, b):
    rows, d = cc.shape
    n = w.shape[1]
    tn = 1536
    return pl.pallas_call(
        _ada_kernel,
        grid=(n // tn,),
        in_specs=[pl.BlockSpec((rows, d), lambda j: (0, 0)),
                  pl.BlockSpec((d, tn), lambda j: (0, j)),
                  pl.BlockSpec((1, tn), lambda j: (0, j))],
        out_specs=pl.BlockSpec((rows, tn), lambda j: (0, j)),
        out_shape=jax.ShapeDtypeStruct((rows, n), F32),
        compiler_params=pltpu.CompilerParams(dimension_semantics=("arbitrary",), vmem_limit_bytes=VMEM_LIMIT),
        name="ada",
    )(cc, w, b)


def _lane_masks():
    lane = lax.broadcasted_iota(jnp.int32, (1, LANES), 1)
    return lane < GLA_DK, lane >= GLA_DK


def _stack_pair(a, m0, m1):
    zero = jnp.zeros_like(a)
    return jnp.concatenate([jnp.where(m0, a, zero), jnp.where(m1, a, zero)], axis=0)


def _state_update(st_ref, dl, grp, kt, v):
    m0, m1 = _lane_masks()
    ktg = kt[:, grp * LANES:(grp + 1) * LANES]
    dlg = dl[:, grp * LANES:(grp + 1) * LANES]
    if grp < GLA_HEADS // 2:
        h0 = 2 * grp
        vs = jnp.concatenate([v[:, h0 * GLA_DV:(h0 + 1) * GLA_DV], v[:, (h0 + 1) * GLA_DV:(h0 + 2) * GLA_DV]], axis=0)
        ks = _stack_pair(ktg, m0, m1)
    else:
        h = grp - GLA_HEADS // 2
        vs = v[:, GLA_V + h * HGRN_DV:GLA_V + (h + 1) * HGRN_DV]
        ks = ktg
    st_ref[grp] = dlg * st_ref[grp] + _dot_tn(vs, ks)


def _seg_cumsum(x, reverse):
    row = lax.broadcasted_iota(jnp.int32, (SUBLANES, 1), 0)
    groups = [x[r:r + SUBLANES] for r in range(0, CHUNK, SUBLANES)]
    out = []
    carry = None
    for g in (reversed(groups) if reverse else groups):
        s = 1
        while s < SUBLANES:
            if reverse:
                g = g + jnp.where(row < SUBLANES - s, pltpu.roll(g, SUBLANES - s, 0), 0.0)
            else:
                g = g + jnp.where(row >= s, pltpu.roll(g, s, 0), 0.0)
            s *= 2
        if carry is not None:
            g = g + carry
        carry = g[0:1] if reverse else g[SUBLANES - 1:SUBLANES]
        out.append(g)
    return jnp.concatenate(out[::-1] if reverse else out, axis=0)


def _ffn_proj_kernel(latent, *refs):
    if latent:
        (x_ref, pos_ref, mod_ref, lng_ref, lnb_ref, w1i_ref, w1o_ref, wmix_ref, a2_ref, ab_ref, lbl_ref, sb0_ref,
         x1_ref, ff_ref, fb_ref, v_ref, g_ref, dlf_ref, dlb_ref, sbo_ref, a_scr, kb_scr, sb_scr) = refs
    else:
        (x_ref, mod_ref, lng_ref, lnb_ref, w1i_ref, w1o_ref, wmix_ref, a2_ref, ab_ref, lbl_ref,
         ff_ref, fb_ref, v_ref, dlf_ref, dlb_ref, a_scr) = refs
    tm = x_ref.shape[1]
    m = mod_ref[0]
    x = x_ref[0]
    if latent:
        x = x + pos_ref[...]
    h = (_ln(x) * (1.0 + m[0:1]) + m[1:2]).astype(BF16)
    y = _swiglu(h, w1i_ref, w1o_ref, a_scr)
    x1 = _post_norm(x, y, m[2:3], FFN_HALF, lng_ref[0:1], lnb_ref[0:1])
    if latent:
        x1_ref[0] = x1
    h2 = (_ln(x1) * (1.0 + m[3:4]) + m[4:5]).astype(BF16)

    def proj(c0, width):
        return _dot(h2, wmix_ref[:, c0:c0 + width])

    lr = proj(_C_LR, LANES).astype(BF16)
    z = _dot(lr, a2_ref[...]) + ab_ref[...]
    g_gla = (jnp.minimum(z, 0.0) - jnp.log(1.0 + jnp.exp(-jnp.abs(z)))) * (1.0 / GLA_GATE_NORMALIZER)
    k_gla = proj(_C_GK, GLA_QK)
    v_ref[0, :, 0:GLA_V] = proj(_C_GV, GLA_V).astype(BF16)
    v_ref[0, :, GLA_V:GLA_V + HGRN_V] = proj(_C_RI, HGRN_V).astype(BF16)
    if latent:
        q_gla = proj(_C_GQ, GLA_QK) * (GLA_DK ** -0.5)
        q_h = _silu(proj(_C_RQ, HGRN_K)) * (HGRN_DK ** -0.5)
        g_ref[0, :, 0:GLA_V] = proj(_C_GG, GLA_V).astype(BF16)
        g_ref[0, :, GLA_V:GLA_V + HGRN_V] = proj(_C_RG, HGRN_V).astype(BF16)

    lbl = lbl_ref[...]
    n_lev = lbl.shape[1]

    for dirn, (c_rf, f_ref, dl_ref) in enumerate(((_C_RFF, ff_ref, dlf_ref), (_C_RFB, fb_ref, dlb_ref))):
        rows = [lbl[dirn, i:i + 1, :] for i in range(n_lev)]
        mx = functools.reduce(jnp.maximum, rows)
        es = [jnp.exp(r - mx) for r in rows]
        lb = es[0] / functools.reduce(lambda a, b: a + b, es)
        f = lb + (1.0 - lb) * _sigmoid(proj(c_rf, HGRN_K))
        logf = jnp.log(f)
        kk = 1.0 - f
        g_d = g_gla[:, dirn * GLA_QK:(dirn + 1) * GLA_QK]
        run = None
        for c in range(tm // CHUNK):
            r0, r1 = c * CHUNK, (c + 1) * CHUNK
            b = _seg_cumsum(jnp.concatenate([g_d[r0:r1], logf[r0:r1]], axis=1), reverse=(dirn == 1))
            bl = b[0:1] if dirn == 1 else b[CHUNK - 1:CHUNK]
            kcat = jnp.concatenate([k_gla[r0:r1], kk[r0:r1]], axis=1)
            ktail = kcat * jnp.exp(bl - b)
            dl = jnp.exp(bl)
            if latent:
                qcat = jnp.concatenate([q_gla[r0:r1], q_h[r0:r1]], axis=1)
                f_ref[0, r0:r1, 0:DECAY_W] = (qcat * jnp.exp(b)).astype(BF16)
                f_ref[0, r0:r1, DECAY_W:2 * DECAY_W] = (kcat * jnp.exp(-b)).astype(BF16)
                f_ref[0, r0:r1, 2 * DECAY_W:3 * DECAY_W] = ktail.astype(BF16)
                if dirn == 1:
                    kb_scr[r0:r1, :] = (ktail if run is None else ktail * run).astype(BF16)
                    run = dl if run is None else run * dl
            else:
                f_ref[0, r0:r1, :] = ktail.astype(BF16)
            dl_ref[0, c:c + 1, :] = dl

    if latent:
        @pl.when(pl.program_id(1) == 0)
        def _():
            sb_scr[...] = sb0_ref[0]

        sbo_ref[0, 0] = sb_scr[...]
        kb = kb_scr[...]
        vv = v_ref[0]
        for grp in range(N_GROUPS):
            _state_update(sb_scr, run, grp, kb, vv)


def _const_spec(shape):
    nd = len(shape)
    return pl.BlockSpec(shape, lambda *_: (0,) * nd, pipeline_mode=pl.Buffered(1))


def _ffn_proj_call(latent, x, pos, mod, mod_row0, lng, lnb, w1i, w1o, wmix, a2, ab, lbl, sb0, tm):
    bsz, t, d = x.shape
    nt = t // tm
    d_ff = w1o.shape[0]
    cpt = tm // CHUNK
    tile = (lambda b, i: (b, nt - 1 - i, 0)) if latent else (lambda b, i: (b, i, 0))
    tok = lambda w: pl.BlockSpec((1, tm, w), tile)
    in_specs = [tok(d)]
    args = [x]
    if latent:
        in_specs.append(pl.BlockSpec((tm, d), lambda b, i: (nt - 1 - i, 0)))
        args.append(pos)
        mod_map = lambda b, i: (b, 0, 0)
    else:
        mod_map = lambda b, i: (mod_row0, 0, 0)
    in_specs += [pl.BlockSpec((1, N_MOD, d), mod_map),
                 _const_spec(lng.shape), _const_spec(lnb.shape), _const_spec(w1i.shape), _const_spec(w1o.shape),
                 _const_spec(wmix.shape), _const_spec(a2.shape), _const_spec(ab.shape), _const_spec(lbl.shape)]
    args += [mod, lng, lnb, w1i, w1o, wmix, a2, ab, lbl]
    fw = 3 * DECAY_W if latent else DECAY_W
    dl_spec = pl.BlockSpec((1, cpt, DECAY_W), tile)
    dl_shape = jax.ShapeDtypeStruct((bsz, t // CHUNK, DECAY_W), F32)
    f_shape = jax.ShapeDtypeStruct((bsz, t, fw), BF16)
    v_shape = jax.ShapeDtypeStruct((bsz, t, GLA_V + HGRN_V), BF16)
    scratch = [pltpu.VMEM((tm, d_ff), BF16)]
    if latent:
        s_block = (1, N_GROUPS, LANES, LANES)
        in_specs.append(pl.BlockSpec(s_block, lambda b, i: (b, 0, 0, 0)))
        args.append(sb0)
        out_specs = [tok(d), tok(fw), tok(fw), tok(GLA_V + HGRN_V), tok(GLA_V + HGRN_V), dl_spec, dl_spec,
                     pl.BlockSpec((1,) + s_block, lambda b, i: (b, nt - 1 - i, 0, 0, 0))]
        out_shape = [jax.ShapeDtypeStruct((bsz, t, d), F32), f_shape, f_shape, v_shape, v_shape, dl_shape, dl_shape,
                     jax.ShapeDtypeStruct((bsz, nt) + s_block[1:], F32)]
        scratch += [pltpu.VMEM((tm, DECAY_W), BF16), pltpu.VMEM(s_block[1:], F32)]
    else:
        out_specs = [tok(fw), tok(fw), tok(GLA_V + HGRN_V), dl_spec, dl_spec]
        out_shape = [f_shape, f_shape, v_shape, dl_shape, dl_shape]
    return pl.pallas_call(
        functools.partial(_ffn_proj_kernel, latent),
        grid=(bsz, nt),
        in_specs=in_specs,
        out_specs=out_specs,
        out_shape=out_shape,
        scratch_shapes=scratch,
        compiler_params=pltpu.CompilerParams(dimension_semantics=("arbitrary", "arbitrary"),
                                             vmem_limit_bytes=VMEM_LIMIT),
        name="ffn_proj_latent" if latent else "ffn_proj_ctx",
    )(*args)


def _ctx_state_kernel(ktf_ref, ktb_ref, v_ref, dlf_ref, dlb_ref, sf_ref, sb_ref):
    n_chunks = v_ref.shape[1] // CHUNK
    sf_ref[...] = jnp.zeros_like(sf_ref)
    sb_ref[...] = jnp.zeros_like(sb_ref)
    for i in range(n_chunks):
        for kt_ref, dl_ref, s_ref, c in ((ktf_ref, dlf_ref, sf_ref, i), (ktb_ref, dlb_ref, sb_ref, n_chunks - 1 - i)):
            kt = kt_ref[0, c * CHUNK:(c + 1) * CHUNK, :]
            v = v_ref[0, c * CHUNK:(c + 1) * CHUNK, :]
            dl = dl_ref[0, c:c + 1, :]
            for grp in range(N_GROUPS):
                _state_update(s_ref.at[0], dl, grp, kt, v)


def _ctx_state_call(ktf, ktb, v, dlf, dlb):
    bsz, tc, _ = v.shape
    full = lambda a: pl.BlockSpec((1,) + a.shape[1:], lambda b: (b,) + (0,) * (a.ndim - 1))
    s_shape = jax.ShapeDtypeStruct((bsz, N_GROUPS, LANES, LANES), F32)
    s_spec = pl.BlockSpec((1, N_GROUPS, LANES, LANES), lambda b: (b, 0, 0, 0))
    return pl.pallas_call(
        _ctx_state_kernel,
        grid=(bsz,),
        in_specs=[full(ktf), full(ktb), full(v), full(dlf), full(dlb)],
        out_specs=[s_spec, s_spec],
        out_shape=[s_shape, s_shape],
        compiler_params=pltpu.CompilerParams(dimension_semantics=("arbitrary",), vmem_limit_bytes=VMEM_LIMIT),
        name="ctx_state",
    )(ktf, ktb, v, dlf, dlb)


def _scan_kernel(ff_ref, fb_ref, v_ref, dlf_ref, dlb_ref, sb_ref, sf0_ref, o_ref, sf_scr):
    tb = v_ref.shape[1]
    nc = tb // CHUNK

    @pl.when(pl.program_id(1) == 0)
    def _():
        for h in range(N_HEADS):
            sf_scr[h] = sf0_ref[0, h // 2 if h < GLA_HEADS else h - GLA_HEADS // 2]

    ii = lax.broadcasted_iota(jnp.int32, (tb, tb), 0)
    jj = lax.broadcasted_iota(jnp.int32, (tb, tb), 1)
    same = (ii >> CHUNK_SHIFT) == (jj >> CHUNK_SHIFT)
    mask_f = same & (jj <= ii)
    mask_b = same & (jj >= ii)
    tok_chunk = lax.broadcasted_iota(jnp.int32, (1, tb), 1) >> CHUNK_SHIFT
    m_pair = _lane_masks()

    for h in range(N_HEADS):
        grp = h // 2 if h < GLA_HEADS else h - GLA_HEADS // 2
        lo = grp * LANES
        vcol = h * LANES
        qf = ff_ref[0, :, lo:lo + LANES]
        qb = fb_ref[0, :, lo:lo + LANES]
        if h < GLA_HEADS:
            qf = jnp.where(m_pair[h % 2], qf, jnp.zeros_like(qf))
            qb = jnp.where(m_pair[h % 2], qb, jnp.zeros_like(qb))
        kif = ff_ref[0, :, DECAY_W + lo:DECAY_W + lo + LANES]
        kib = fb_ref[0, :, DECAY_W + lo:DECAY_W + lo + LANES]
        ktf = ff_ref[0, :, 2 * DECAY_W + lo:2 * DECAY_W + lo + LANES]
        ktb = fb_ref[0, :, 2 * DECAY_W + lo:2 * DECAY_W + lo + LANES]
        vh = v_ref[0, :, vcol:vcol + LANES]

        vt = vh.T
        zero = jnp.zeros_like(vt)
        lhs = jnp.concatenate([jnp.where(tok_chunk == c, vt, zero) for c in range(nc)], axis=0)
        u = _dot(lhs, jnp.concatenate([ktf, ktb], axis=1))

        st_f = [None] * nc
        cur = sf_scr[h]
        for c in range(nc):
            st_f[c] = cur.astype(BF16)
            cur = dlf_ref[0, c:c + 1, lo:lo + LANES] * cur + u[c * LANES:(c + 1) * LANES, 0:LANES]
        sf_scr[h] = cur
        st_b = [None] * nc
        cur = sb_ref[0, 0, grp]
        for c in reversed(range(nc)):
            st_b[c] = cur.astype(BF16)
            cur = dlb_ref[0, c:c + 1, lo:lo + LANES] * cur + u[c * LANES:(c + 1) * LANES, LANES:2 * LANES]

        att = jnp.where(mask_f, _dot_nt(qf, kif), 0.0) + jnp.where(mask_b, _dot_nt(qb, kib), 0.0)
        o = _dot(att.astype(BF16), vh)
        for c in range(nc):
            r0, r1 = c * CHUNK, (c + 1) * CHUNK
            qc = jnp.concatenate([qf[r0:r1], qb[r0:r1]], axis=1)
            sc = jnp.concatenate([st_f[c], st_b[c]], axis=1)
            o_ref[0, r0:r1, vcol:vcol + LANES] = o[r0:r1] + _dot_nt(qc, sc)


def _scan_call(ff, fb, v, dlf, dlb, sb, sf0, tb):
    bsz, t, _ = v.shape
    nt = t // tb
    cpt = tb // CHUNK
    tok = lambda w: pl.BlockSpec((1, tb, w), lambda b, i: (b, i, 0))
    s_block = (1, N_GROUPS, LANES, LANES)
    return pl.pallas_call(
        _scan_kernel,
        grid=(bsz, nt),
        in_specs=[tok(ff.shape[2]), tok(fb.shape[2]), tok(v.shape[2]),
                  pl.BlockSpec((1, cpt, DECAY_W), lambda b, i: (b, i, 0)),
                  pl.BlockSpec((1, cpt, DECAY_W), lambda b, i: (b, i, 0)),
                  pl.BlockSpec((1,) + s_block, lambda b, i: (b, i, 0, 0, 0)),
                  pl.BlockSpec(s_block, lambda b, i: (b, 0, 0, 0))],
        out_specs=tok(GLA_V + HGRN_V),
        out_shape=jax.ShapeDtypeStruct((bsz, t, GLA_V + HGRN_V), F32),
        scratch_shapes=[pltpu.VMEM((N_HEADS, LANES, LANES), F32)],
        compiler_params=pltpu.CompilerParams(dimension_semantics=("arbitrary", "arbitrary"),
                                             vmem_limit_bytes=VMEM_LIMIT),
        name="scan",
    )(ff, fb, v, dlf, dlb, sb, sf0)


def _out_ffn_kernel(x1_ref, o_ref, g_ref, mod_ref, lng_ref, lnb_ref, hg_ref, wo_ref, w2i_ref, w2o_ref,
                    out_ref, a_scr, mg_scr, y_scr):
    m = mod_ref[0]
    for r0 in range(0, x1_ref.shape[1], SUB_TILE):
        rows = slice(r0, r0 + SUB_TILE)
        for h in range(N_HEADS):
            lanes = slice(h * LANES, (h + 1) * LANES)
            oh = o_ref[0, rows, lanes]
            ms = jnp.mean(oh * oh, axis=-1, keepdims=True)
            gate = g_ref[0, rows, lanes].astype(F32)
            mg_scr[:, lanes] = (oh * lax.rsqrt(ms + NORM_EPS) * hg_ref[:, lanes] * _silu(gate)).astype(BF16)
        y = _dot(mg_scr[...], wo_ref[...])
        x2 = _post_norm(x1_ref[0, rows], y, m[5:6], 1.0, lng_ref[1:2], lnb_ref[1:2])
        h3 = (_ln(x2) * (1.0 + m[6:7]) + m[7:8]).astype(BF16)
        y_scr[...] = _swiglu(h3, w2i_ref, w2o_ref, a_scr)
        out_ref[0, rows] = _post_norm(x2, y_scr[...], m[8:9], FFN_HALF, lng_ref[2:3], lnb_ref[2:3])


def _out_ffn_call(x1, o, g, mod, lng, lnb, hgain, wo, w2i, w2o, tm):
    bsz, t, d = x1.shape
    nt = t // tm
    d_ff = w2o.shape[0]
    tok = lambda w: pl.BlockSpec((1, tm, w), lambda b, i: (b, i, 0))
    return pl.pallas_call(
        _out_ffn_kernel,
        grid=(bsz, nt),
        in_specs=[tok(d), tok(o.shape[2]), tok(g.shape[2]),
                  pl.BlockSpec((1, N_MOD, d), lambda b, i: (b, 0, 0)),
                  _const_spec(lng.shape), _const_spec(lnb.shape), _const_spec(hgain.shape),
                  _const_spec(wo.shape), _const_spec(w2i.shape), _const_spec(w2o.shape)],
        out_specs=tok(d),
        out_shape=jax.ShapeDtypeStruct((bsz, t, d), F32),
        scratch_shapes=[pltpu.VMEM((SUB_TILE, d_ff), BF16), pltpu.VMEM((SUB_TILE, o.shape[2]), BF16),
                        pltpu.VMEM((SUB_TILE, d), F32)],
        compiler_params=pltpu.CompilerParams(dimension_semantics=("arbitrary", "arbitrary"),
                                             vmem_limit_bytes=VMEM_LIMIT),
        name="out_ffn",
    )(x1, o, g, mod, lng, lnb, hgain, wo, w2i, w2o)


def _sincos_2d(rows, width, dim):
    r = jnp.repeat(jnp.arange(rows), width)
    col = jnp.tile(jnp.arange(width), rows)
    quarter = dim // 4
    omega = 1.0 / POS_THETA ** (jnp.arange(quarter, dtype=F32) / quarter)

    def emb(p):
        a = p.astype(F32)[:, None] * omega[None, :]
        return jnp.concatenate([jnp.sin(a), jnp.cos(a)], axis=-1)

    return jnp.concatenate([emb(r), emb(col)], axis=-1)


def _mix_weights(w_mix_in, a2_f, a2_b, ab_f, ab_b):
    d = w_mix_in.shape[0]
    o = np.cumsum((0, GLA_QK, GLA_QK, GLA_V, GLA_V, GLA_GATE_RANK, GLA_GATE_RANK, HGRN_K, HGRN_K, HGRN_K, HGRN_V, HGRN_V))
    seg = lambda i: w_mix_in[:, o[i]:o[i + 1]]
    pad = jnp.zeros((d, LANES - 2 * GLA_GATE_RANK), w_mix_in.dtype)
    wmix = jnp.concatenate([seg(0), seg(1), seg(2), seg(3), seg(6), seg(7), seg(8), seg(9), seg(10), seg(4), seg(5), pad],
                           axis=1).astype(BF16)
    a2 = jnp.zeros((LANES, 2 * GLA_QK), F32)
    a2 = a2.at[0:GLA_GATE_RANK, 0:GLA_QK].set(a2_f).at[GLA_GATE_RANK:2 * GLA_GATE_RANK, GLA_QK:].set(a2_b)
    ab = jnp.concatenate([ab_f, ab_b])[None, :]
    return wmix, a2.astype(BF16), ab


def kernel(x, c, ctx, c_ctx, w_ada, b_ada, ln_gain, ln_bias, ffn1_w_in, ffn1_w_out, w_mix_in, gla_a2_fwd, gla_a2_bwd,
           gla_a_bias_fwd, gla_a_bias_bwd, hgrn_lb_logits, gla_norm_gain, hgrn_norm_gain, w_mix_out, ffn2_w_in,
           ffn2_w_out):
    bsz, t, d = x.shape
    tm = min(256, t)
    tmc = min(256, ctx.shape[1])
    pos = _sincos_2d(t // GRID_W, GRID_W, d).astype(x.dtype)

    mod_rows = -(-(bsz + 1) // 8) * 8
    cc = jnp.zeros((mod_rows, d), F32).at[:bsz].set(c).at[bsz].set(c_ctx)
    mod = _ada_call(cc, w_ada[0], b_ada[0][None, :]).reshape(mod_rows, N_MOD, d)

    w1i, w1o = ffn1_w_in[0].astype(BF16), ffn1_w_out[0].astype(BF16)
    w2i, w2o = ffn2_w_in[0].astype(BF16), ffn2_w_out[0].astype(BF16)
    wmix, a2, ab = _mix_weights(w_mix_in[0], gla_a2_fwd[0], gla_a2_bwd[0], gla_a_bias_fwd[0], gla_a_bias_bwd[0])
    lng, lnb = ln_gain[0], ln_bias[0]
    hgain = jnp.concatenate([jnp.tile(gla_norm_gain[0], GLA_HEADS), jnp.tile(hgrn_norm_gain[0], HGRN_HEADS)])[None, :]

    ktf_c, ktb_c, v_c, dlf_c, dlb_c = _ffn_proj_call(False, ctx, None, mod, bsz, lng, lnb, w1i, w1o, wmix, a2, ab,
                                                     hgrn_lb_logits, None, tmc)
    sf0, sb0 = _ctx_state_call(ktf_c, ktb_c, v_c, dlf_c, dlb_c)
    x1, ff, fb, v, g, dlf, dlb, sb = _ffn_proj_call(True, x, pos, mod, bsz, lng, lnb, w1i, w1o, wmix, a2, ab,
                                                    hgrn_lb_logits, sb0, tm)
    o = _scan_call(ff, fb, v, dlf, dlb, sb, sf0, tm)
    return _out_ffn_call(x1, o, g, mod, lng, lnb, hgain, w_mix_out[0].astype(BF16), w2i, w2o, min(2 * SUB_TILE, t))
```

```python
import functools

import numpy as np
import jax
import jax.numpy as jnp
from jax import lax
from jax.experimental import pallas as pl
from jax.experimental.pallas import tpu as pltpu

N_SUBLAYERS = 3
N_MOD = 3 * N_SUBLAYERS
FFN_HALF = 0.5
GLA_HEADS = 4
GLA_DK = 64
GLA_DV = 128
GLA_GATE_RANK = 16
GLA_GATE_NORMALIZER = 16.0
HGRN_HEADS = 4
HGRN_DK = 128
HGRN_DV = 128
GLA_QK = GLA_HEADS * GLA_DK
GLA_V = GLA_HEADS * GLA_DV
HGRN_K = HGRN_HEADS * HGRN_DK
HGRN_V = HGRN_HEADS * HGRN_DV
CHUNK = 64
HALF = CHUNK // 2
CHUNK_SHIFT = 6
LN_EPS = 1e-5
NORM_EPS = 1e-6
POS_THETA = 10000.0
GRID_W = 64
DEPTH = 1
DN_ALPHA = (2.0 * DEPTH) ** 0.25

LANES = 128
SUBLANES = 8
FF_CHUNK = 256
SUB_TILE = 256
DECAY_W = GLA_QK + HGRN_K
N_GROUPS = GLA_HEADS // 2 + HGRN_HEADS
N_HEADS = GLA_HEADS + HGRN_HEADS
VMEM_LIMIT = 56 * 1024 * 1024

_C_GQ, _C_GK, _C_GV, _C_GG = 0, 256, 512, 1024
_C_RQ, _C_RFF, _C_RFB, _C_RI, _C_RG, _C_LR = 1536, 2048, 2560, 3072, 3584, 4096
MIX_W = 4096 + LANES

F32 = jnp.float32
BF16 = jnp.bfloat16


def _dot(a, b):
    return jnp.dot(a, b, preferred_element_type=F32)


def _dot_nt(a, b):
    return lax.dot_general(a, b, (((1,), (1,)), ((), ())), preferred_element_type=F32)


def _dot_tn(a, b):
    return lax.dot_general(a, b, (((0,), (0,)), ((), ())), preferred_element_type=F32)


def _sigmoid(x):
    return 1.0 / (1.0 + jnp.exp(-x))


def _silu(x):
    return x * _sigmoid(x)


def _ln(x):
    mu = jnp.mean(x, axis=-1, keepdims=True)
    xc = x - mu
    var = jnp.mean(xc * xc, axis=-1, keepdims=True)
    return xc * lax.rsqrt(var + LN_EPS)


def _swiglu(h, w_in_ref, w_out_ref, a_scr):
    d_ff = w_out_ref.shape[0]
    for c0 in range(0, d_ff, FF_CHUNK):
        g = _dot(h, w_in_ref[:, c0:c0 + FF_CHUNK])
        u = _dot(h, w_in_ref[:, d_ff + c0:d_ff + c0 + FF_CHUNK])
        a_scr[:, c0:c0 + FF_CHUNK] = (_silu(g) * u).astype(BF16)
    return _dot(a_scr[...], w_out_ref[...])


def _post_norm(x, y, gate, weight, gain, bias):
    return _ln(DN_ALPHA * x + (weight * gate) * y) * gain + bias


def _ada_kernel(c_ref, w_ref, b_ref, o_ref):
    s = _silu(c_ref[...]).astype(BF16)
    o_ref[...] = _dot(s, w_ref[...].astype(BF16)) + b_ref[...]


def _ada_call(cc, w, b):
    rows, d = cc.shape
    n = w.shape[1]
    tn = 1536
    return pl.pallas_call(
        _ada_kernel,
        grid=(n // tn,),
        in_specs=[pl.BlockSpec((rows, d), lambda j: (0, 0)),
                  pl.BlockSpec((d, tn), lambda j: (0, j)),
                  pl.BlockSpec((1, tn), lambda j: (0, j))],
        out_specs=pl.BlockSpec((rows, tn), lambda j: (0, j)),
        out_shape=jax.ShapeDtypeStruct((rows, n), F32),
        compiler_params=pltpu.CompilerParams(dimension_semantics=("arbitrary",), vmem_limit_bytes=VMEM_LIMIT),
        name="ada",
    )(cc, w, b)


def _lane_masks():
    lane = lax.broadcasted_iota(jnp.int32, (1, LANES), 1)
    return lane < GLA_DK, lane >= GLA_DK


def _stack_pair(a, m0, m1):
    zero = jnp.zeros_like(a)
    return jnp.concatenate([jnp.where(m0, a, zero), jnp.where(m1, a, zero)], axis=0)


def _state_update(st_ref, dl, grp, kt, v):
    m0, m1 = _lane_masks()
    ktg = kt[:, grp * LANES:(grp + 1) * LANES]
    dlg = dl[:, grp * LANES:(grp + 1) * LANES]
    if grp < GLA_HEADS // 2:
        h0 = 2 * grp
        vs = jnp.concatenate([v[:, h0 * GLA_DV:(h0 + 1) * GLA_DV], v[:, (h0 + 1) * GLA_DV:(h0 + 2) * GLA_DV]], axis=0)
        ks = _stack_pair(ktg, m0, m1)
    else:
        h = grp - GLA_HEADS // 2
        vs = v[:, GLA_V + h * HGRN_DV:GLA_V + (h + 1) * HGRN_DV]
        ks = ktg
    st_ref[grp] = dlg * st_ref[grp] + _dot_tn(vs, ks)


def _seg_cumsum(x, reverse):
    row = lax.broadcasted_iota(jnp.int32, (SUBLANES, 1), 0)
    groups = [x[r:r + SUBLANES] for r in range(0, CHUNK, SUBLANES)]
    out = []
    carry = None
    for g in (reversed(groups) if reverse else groups):
        s = 1
        while s < SUBLANES:
            if reverse:
                g = g + jnp.where(row < SUBLANES - s, pltpu.roll(g, SUBLANES - s, 0), 0.0)
            else:
                g = g + jnp.where(row >= s, pltpu.roll(g, s, 0), 0.0)
            s *= 2
        if carry is not None:
            g = g + carry
        carry = g[0:1] if reverse else g[SUBLANES - 1:SUBLANES]
        out.append(g)
    return jnp.concatenate(out[::-1] if reverse else out, axis=0)


def _ffn_proj_kernel(latent, *refs):
    if latent:
        (x_ref, pos_ref, mod_ref, lng_ref, lnb_ref, w1i_ref, w1o_ref, wmix_ref, a2_ref, ab_ref, lbl_ref, sb0_ref,
         x1_ref, ff_ref, fb_ref, v_ref, g_ref, dlf_ref, dlb_ref, sbo_ref, a_scr, kb_scr, sb_scr) = refs
    else:
        (x_ref, mod_ref, lng_ref, lnb_ref, w1i_ref, w1o_ref, wmix_ref, a2_ref, ab_ref, lbl_ref,
         ff_ref, fb_ref, v_ref, dlf_ref, dlb_ref, a_scr) = refs
    tm = x_ref.shape[1]
    cpt = tm // CHUNK
    m = mod_ref[0]
    x = x_ref[0]
    if latent:
        x = x + pos_ref[...]
    h = (_ln(x) * (1.0 + m[0:1]) + m[1:2]).astype(BF16)
    y = _swiglu(h, w1i_ref, w1o_ref, a_scr)
    x1 = _post_norm(x, y, m[2:3], FFN_HALF, lng_ref[0:1], lnb_ref[0:1])
    if latent:
        x1_ref[0] = x1
    h2 = (_ln(x1) * (1.0 + m[3:4]) + m[4:5]).astype(BF16)

    def proj(c0, width):
        return _dot(h2, wmix_ref[:, c0:c0 + width])

    lr = proj(_C_LR, LANES).astype(BF16)
    z = _dot(lr, a2_ref[...]) + ab_ref[...]
    g_gla = (jnp.minimum(z, 0.0) - jnp.log(1.0 + jnp.exp(-jnp.abs(z)))) * (1.0 / GLA_GATE_NORMALIZER)
    k_gla = proj(_C_GK, GLA_QK)
    v_ref[0, :, 0:GLA_V] = proj(_C_GV, GLA_V).astype(BF16)
    v_ref[0, :, GLA_V:GLA_V + HGRN_V] = proj(_C_RI, HGRN_V).astype(BF16)
    if latent:
        q_gla = proj(_C_GQ, GLA_QK) * (GLA_DK ** -0.5)
        q_h = _silu(proj(_C_RQ, HGRN_K)) * (HGRN_DK ** -0.5)
        g_ref[0, :, 0:GLA_V] = proj(_C_GG, GLA_V).astype(BF16)
        g_ref[0, :, GLA_V:GLA_V + HGRN_V] = proj(_C_RG, HGRN_V).astype(BF16)

    lbl = lbl_ref[...]
    n_lev = lbl.shape[1]

    for dirn, (c_rf, f_ref, dl_ref) in enumerate(((_C_RFF, ff_ref, dlf_ref), (_C_RFB, fb_ref, dlb_ref))):
        rows = [lbl[dirn, i:i + 1, :] for i in range(n_lev)]
        mx = functools.reduce(jnp.maximum, rows)
        es = [jnp.exp(r - mx) for r in rows]
        lb = es[0] / functools.reduce(lambda a, b: a + b, es)
        f = lb + (1.0 - lb) * _sigmoid(proj(c_rf, HGRN_K))
        logf = jnp.log(f)
        kk = 1.0 - f
        g_d = g_gla[:, dirn * GLA_QK:(dirn + 1) * GLA_QK]
        run = None
        for c in range(cpt):
            r0, r1 = c * CHUNK, (c + 1) * CHUNK
            b = _seg_cumsum(jnp.concatenate([g_d[r0:r1], logf[r0:r1]], axis=1), reverse=(dirn == 1))
            bl = b[0:1] if dirn == 1 else b[CHUNK - 1:CHUNK]
            bm = b[HALF:HALF + 1] if dirn == 1 else b[HALF - 1:HALF]
            kcat = jnp.concatenate([k_gla[r0:r1], kk[r0:r1]], axis=1)
            ktail = kcat * jnp.exp(bl - b)
            dl = jnp.exp(bl)
            if latent:
                qcat = jnp.concatenate([q_gla[r0:r1], q_h[r0:r1]], axis=1)
                f_ref[0, r0:r1, 0:DECAY_W] = (qcat * jnp.exp(b - bm)).astype(BF16)
                f_ref[0, r0:r1, DECAY_W:2 * DECAY_W] = (kcat * jnp.exp(bm - b)).astype(BF16)
                f_ref[0, r0:r1, 2 * DECAY_W:3 * DECAY_W] = ktail.astype(BF16)
                if dirn == 1:
                    kb_scr[r0:r1, :] = (ktail if run is None else ktail * run).astype(BF16)
                    run = dl if run is None else run * dl
            else:
                f_ref[0, r0:r1, :] = ktail.astype(BF16)
            dl_ref[0, c:c + 1, :] = dl
            dl_ref[0, cpt + c:cpt + c + 1, :] = jnp.exp(bm)

    if latent:
        @pl.when(pl.program_id(1) == 0)
        def _():
            sb_scr[...] = sb0_ref[0]

        sbo_ref[0, 0] = sb_scr[...]
        kb = kb_scr[...]
        vv = v_ref[0]
        for grp in range(N_GROUPS):
            _state_update(sb_scr, run, grp, kb, vv)


def _const_spec(shape):
    nd = len(shape)
    return pl.BlockSpec(shape, lambda *_: (0,) * nd, pipeline_mode=pl.Buffered(1))


def _ffn_proj_call(latent, x, pos, mod, mod_row0, lng, lnb, w1i, w1o, wmix, a2, ab, lbl, sb0, tm):
    bsz, t, d = x.shape
    nt = t // tm
    d_ff = w1o.shape[0]
    cpt = tm // CHUNK
    tile = (lambda b, i: (b, nt - 1 - i, 0)) if latent else (lambda b, i: (b, i, 0))
    tok = lambda w: pl.BlockSpec((1, tm, w), tile)
    in_specs = [tok(d)]
    args = [x]
    if latent:
        in_specs.append(pl.BlockSpec((tm, d), lambda b, i: (nt - 1 - i, 0)))
        args.append(pos)
        mod_map = lambda b, i: (b, 0, 0)
    else:
        mod_map = lambda b, i: (mod_row0, 0, 0)
    in_specs += [pl.BlockSpec((1, N_MOD, d), mod_map),
                 _const_spec(lng.shape), _const_spec(lnb.shape), _const_spec(w1i.shape), _const_spec(w1o.shape),
                 _const_spec(wmix.shape), _const_spec(a2.shape), _const_spec(ab.shape), _const_spec(lbl.shape)]
    args += [mod, lng, lnb, w1i, w1o, wmix, a2, ab, lbl]
    fw = 3 * DECAY_W if latent else DECAY_W
    dl_spec = pl.BlockSpec((1, 2 * cpt, DECAY_W), tile)
    dl_shape = jax.ShapeDtypeStruct((bsz, 2 * t // CHUNK, DECAY_W), F32)
    f_shape = jax.ShapeDtypeStruct((bsz, t, fw), BF16)
    v_shape = jax.ShapeDtypeStruct((bsz, t, GLA_V + HGRN_V), BF16)
    scratch = [pltpu.VMEM((tm, d_ff), BF16)]
    if latent:
        s_block = (1, N_GROUPS, LANES, LANES)
        in_specs.append(pl.BlockSpec(s_block, lambda b, i: (b, 0, 0, 0)))
        args.append(sb0)
        out_specs = [tok(d), tok(fw), tok(fw), tok(GLA_V + HGRN_V), tok(GLA_V + HGRN_V), dl_spec, dl_spec,
                     pl.BlockSpec((1,) + s_block, lambda b, i: (b, nt - 1 - i, 0, 0, 0))]
        out_shape = [jax.ShapeDtypeStruct((bsz, t, d), F32), f_shape, f_shape, v_shape, v_shape, dl_shape, dl_shape,
                     jax.ShapeDtypeStruct((bsz, nt) + s_block[1:], F32)]
        scratch += [pltpu.VMEM((tm, DECAY_W), BF16), pltpu.VMEM(s_block[1:], F32)]
    else:
        out_specs = [tok(fw), tok(fw), tok(GLA_V + HGRN_V), dl_spec, dl_spec]
        out_shape = [f_shape, f_shape, v_shape, dl_shape, dl_shape]
    return pl.pallas_call(
        functools.partial(_ffn_proj_kernel, latent),
        grid=(bsz, nt),
        in_specs=in_specs,
        out_specs=out_specs,
        out_shape=out_shape,
        scratch_shapes=scratch,
        compiler_params=pltpu.CompilerParams(dimension_semantics=("arbitrary", "arbitrary"),
                                             vmem_limit_bytes=VMEM_LIMIT),
        name="ffn_proj_latent" if latent else "ffn_proj_ctx",
    )(*args)


def _ctx_state_kernel(ktf_ref, ktb_ref, v_ref, dlf_ref, dlb_ref, sf_ref, sb_ref):
    n_chunks = v_ref.shape[1] // CHUNK
    sf_ref[...] = jnp.zeros_like(sf_ref)
    sb_ref[...] = jnp.zeros_like(sb_ref)
    for i in range(n_chunks):
        for kt_ref, dl_ref, s_ref, c in ((ktf_ref, dlf_ref, sf_ref, i), (ktb_ref, dlb_ref, sb_ref, n_chunks - 1 - i)):
            kt = kt_ref[0, c * CHUNK:(c + 1) * CHUNK, :]
            v = v_ref[0, c * CHUNK:(c + 1) * CHUNK, :]
            dl = dl_ref[0, c:c + 1, :]
            for grp in range(N_GROUPS):
                _state_update(s_ref.at[0], dl, grp, kt, v)


def _ctx_state_call(ktf, ktb, v, dlf, dlb):
    bsz, tc, _ = v.shape
    full = lambda a: pl.BlockSpec((1,) + a.shape[1:], lambda b: (b,) + (0,) * (a.ndim - 1))
    s_shape = jax.ShapeDtypeStruct((bsz, N_GROUPS, LANES, LANES), F32)
    s_spec = pl.BlockSpec((1, N_GROUPS, LANES, LANES), lambda b: (b, 0, 0, 0))
    return pl.pallas_call(
        _ctx_state_kernel,
        grid=(bsz,),
        in_specs=[full(ktf), full(ktb), full(v), full(dlf), full(dlb)],
        out_specs=[s_spec, s_spec],
        out_shape=[s_shape, s_shape],
        compiler_params=pltpu.CompilerParams(dimension_semantics=("arbitrary",), vmem_limit_bytes=VMEM_LIMIT),
        name="ctx_state",
    )(ktf, ktb, v, dlf, dlb)


def _scan_kernel(ff_ref, fb_ref, v_ref, dlf_ref, dlb_ref, sb_ref, sf0_ref, o_ref, sf_scr):
    tb = v_ref.shape[1]
    nc = tb // CHUNK

    @pl.when(pl.program_id(1) == 0)
    def _():
        for h in range(N_HEADS):
            sf_scr[h] = sf0_ref[0, h // 2 if h < GLA_HEADS else h - GLA_HEADS // 2]

    ii = lax.broadcasted_iota(jnp.int32, (tb, tb), 0)
    jj = lax.broadcasted_iota(jnp.int32, (tb, tb), 1)
    same = (ii >> CHUNK_SHIFT) == (jj >> CHUNK_SHIFT)
    mask_f = same & (jj <= ii)
    mask_b = same & (jj >= ii)
    tok_chunk = lax.broadcasted_iota(jnp.int32, (1, tb), 1) >> CHUNK_SHIFT
    m_pair = _lane_masks()

    for h in range(N_HEADS):
        grp = h // 2 if h < GLA_HEADS else h - GLA_HEADS // 2
        lo = grp * LANES
        vcol = h * LANES
        qf = ff_ref[0, :, lo:lo + LANES]
        qb = fb_ref[0, :, lo:lo + LANES]
        if h < GLA_HEADS:
            qf = jnp.where(m_pair[h % 2], qf, jnp.zeros_like(qf))
            qb = jnp.where(m_pair[h % 2], qb, jnp.zeros_like(qb))
        kif = ff_ref[0, :, DECAY_W + lo:DECAY_W + lo + LANES]
        kib = fb_ref[0, :, DECAY_W + lo:DECAY_W + lo + LANES]
        ktf = ff_ref[0, :, 2 * DECAY_W + lo:2 * DECAY_W + lo + LANES]
        ktb = fb_ref[0, :, 2 * DECAY_W + lo:2 * DECAY_W + lo + LANES]
        vh = v_ref[0, :, vcol:vcol + LANES]

        vt = vh.T
        zero = jnp.zeros_like(vt)
        lhs = jnp.concatenate([jnp.where(tok_chunk == c, vt, zero) for c in range(nc)], axis=0)
        u = _dot(lhs, jnp.concatenate([ktf, ktb], axis=1))

        st_f = [None] * nc
        cur = sf_scr[h]
        for c in range(nc):
            st_f[c] = (dlf_ref[0, nc + c:nc + c + 1, lo:lo + LANES] * cur).astype(BF16)
            cur = dlf_ref[0, c:c + 1, lo:lo + LANES] * cur + u[c * LANES:(c + 1) * LANES, 0:LANES]
        sf_scr[h] = cur
        st_b = [None] * nc
        cur = sb_ref[0, 0, grp]
        for c in reversed(range(nc)):
            st_b[c] = (dlb_ref[0, nc + c:nc + c + 1, lo:lo + LANES] * cur).astype(BF16)
            cur = dlb_ref[0, c:c + 1, lo:lo + LANES] * cur + u[c * LANES:(c + 1) * LANES, LANES:2 * LANES]

        att = jnp.where(mask_f, _dot_nt(qf, kif), 0.0) + jnp.where(mask_b, _dot_nt(qb, kib), 0.0)
        o = _dot(att.astype(BF16), vh)
        for c in range(nc):
            r0, r1 = c * CHUNK, (c + 1) * CHUNK
            qc = jnp.concatenate([qf[r0:r1], qb[r0:r1]], axis=1)
            sc = jnp.concatenate([st_f[c], st_b[c]], axis=1)
            o_ref[0, r0:r1, vcol:vcol + LANES] = o[r0:r1] + _dot_nt(qc, sc)


def _scan_call(ff, fb, v, dlf, dlb, sb, sf0, tb):
    bsz, t, _ = v.shape
    nt = t // tb
    cpt = tb // CHUNK
    tok = lambda w: pl.BlockSpec((1, tb, w), lambda b, i: (b, i, 0))
    s_block = (1, N_GROUPS, LANES, LANES)
    return pl.pallas_call(
        _scan_kernel,
        grid=(bsz, nt),
        in_specs=[tok(ff.shape[2]), tok(fb.shape[2]), tok(v.shape[2]),
                  pl.BlockSpec((1, 2 * cpt, DECAY_W), lambda b, i: (b, i, 0)),
                  pl.BlockSpec((1, 2 * cpt, DECAY_W), lambda b, i: (b, i, 0)),
                  pl.BlockSpec((1,) + s_block, lambda b, i: (b, i, 0, 0, 0)),
                  pl.BlockSpec(s_block, lambda b, i: (b, 0, 0, 0))],
        out_specs=tok(GLA_V + HGRN_V),
        out_shape=jax.ShapeDtypeStruct((bsz, t, GLA_V + HGRN_V), F32),
        scratch_shapes=[pltpu.VMEM((N_HEADS, LANES, LANES), F32)],
        compiler_params=pltpu.CompilerParams(dimension_semantics=("arbitrary", "arbitrary"),
                                             vmem_limit_bytes=VMEM_LIMIT),
        name="scan",
    )(ff, fb, v, dlf, dlb, sb, sf0)


def _out_ffn_kernel(x1_ref, o_ref, g_ref, mod_ref, lng_ref, lnb_ref, hg_ref, wo_ref, w2i_ref, w2o_ref,
                    out_ref, a_scr, mg_scr, y_scr):
    m = mod_ref[0]
    for r0 in range(0, x1_ref.shape[1], SUB_TILE):
        rows = slice(r0, r0 + SUB_TILE)
        for h in range(N_HEADS):
            lanes = slice(h * LANES, (h + 1) * LANES)
            oh = o_ref[0, rows, lanes]
            ms = jnp.mean(oh * oh, axis=-1, keepdims=True)
            gate = g_ref[0, rows, lanes].astype(F32)
            mg_scr[:, lanes] = (oh * lax.rsqrt(ms + NORM_EPS) * hg_ref[:, lanes] * _silu(gate)).astype(BF16)
        y = _dot(mg_scr[...], wo_ref[...])
        x2 = _post_norm(x1_ref[0, rows], y, m[5:6], 1.0, lng_ref[1:2], lnb_ref[1:2])
        h3 = (_ln(x2) * (1.0 + m[6:7]) + m[7:8]).astype(BF16)
        y_scr[...] = _swiglu(h3, w2i_ref, w2o_ref, a_scr)
        out_ref[0, rows] = _post_norm(x2, y_scr[...], m[8:9], FFN_HALF, lng_ref[2:3], lnb_ref[2:3])


def _out_ffn_call(x1, o, g, mod, lng, lnb, hgain, wo, w2i, w2o, tm):
    bsz, t, d = x1.shape
    nt = t // tm
    d_ff = w2o.shape[0]
    tok = lambda w: pl.BlockSpec((1, tm, w), lambda b, i: (b, i, 0))
    return pl.pallas_call(
        _out_ffn_kernel,
        grid=(bsz, nt),
        in_specs=[tok(d), tok(o.shape[2]), tok(g.shape[2]),
                  pl.BlockSpec((1, N_MOD, d), lambda b, i: (b, 0, 0)),
                  _const_spec(lng.shape), _const_spec(lnb.shape), _const_spec(hgain.shape),
                  _const_spec(wo.shape), _const_spec(w2i.shape), _const_spec(w2o.shape)],
        out_specs=tok(d),
        out_shape=jax.ShapeDtypeStruct((bsz, t, d), F32),
        scratch_shapes=[pltpu.VMEM((SUB_TILE, d_ff), BF16), pltpu.VMEM((SUB_TILE, o.shape[2]), BF16),
                        pltpu.VMEM((SUB_TILE, d), F32)],
        compiler_params=pltpu.CompilerParams(dimension_semantics=("arbitrary", "arbitrary"),
                                             vmem_limit_bytes=VMEM_LIMIT),
        name="out_ffn",
    )(x1, o, g, mod, lng, lnb, hgain, wo, w2i, w2o)


def _sincos_2d(rows, width, dim):
    r = jnp.repeat(jnp.arange(rows), width)
    col = jnp.tile(jnp.arange(width), rows)
    quarter = dim // 4
    omega = 1.0 / POS_THETA ** (jnp.arange(quarter, dtype=F32) / quarter)

    def emb(p):
        a = p.astype(F32)[:, None] * omega[None, :]
        return jnp.concatenate([jnp.sin(a), jnp.cos(a)], axis=-1)

    return jnp.concatenate([emb(r), emb(col)], axis=-1)


def _mix_weights(w_mix_in, a2_f, a2_b, ab_f, ab_b):
    d = w_mix_in.shape[0]
    o = np.cumsum((0, GLA_QK, GLA_QK, GLA_V, GLA_V, GLA_GATE_RANK, GLA_GATE_RANK, HGRN_K, HGRN_K, HGRN_K, HGRN_V, HGRN_V))
    seg = lambda i: w_mix_in[:, o[i]:o[i + 1]]
    pad = jnp.zeros((d, LANES - 2 * GLA_GATE_RANK), w_mix_in.dtype)
    wmix = jnp.concatenate([seg(0), seg(1), seg(2), seg(3), seg(6), seg(7), seg(8), seg(9), seg(10), seg(4), seg(5), pad],
                           axis=1).astype(BF16)
    a2 = jnp.zeros((LANES, 2 * GLA_QK), F32)
    a2 = a2.at[0:GLA_GATE_RANK, 0:GLA_QK].set(a2_f).at[GLA_GATE_RANK:2 * GLA_GATE_RANK, GLA_QK:].set(a2_b)
    ab = jnp.concatenate([ab_f, ab_b])[None, :]
    return wmix, a2.astype(BF16), ab


def kernel(x, c, ctx, c_ctx, w_ada, b_ada, ln_gain, ln_bias, ffn1_w_in, ffn1_w_out, w_mix_in, gla_a2_fwd, gla_a2_bwd,
           gla_a_bias_fwd, gla_a_bias_bwd, hgrn_lb_logits, gla_norm_gain, hgrn_norm_gain, w_mix_out, ffn2_w_in,
           ffn2_w_out):
    bsz, t, d = x.shape
    tm = min(256, t)
    tmc = min(256, ctx.shape[1])
    pos = _sincos_2d(t // GRID_W, GRID_W, d).astype(x.dtype)

    mod_rows = -(-(bsz + 1) // 8) * 8
    cc = jnp.zeros((mod_rows, d), F32).at[:bsz].set(c).at[bsz].set(c_ctx)
    mod = _ada_call(cc, w_ada[0], b_ada[0][None, :]).reshape(mod_rows, N_MOD, d)

    w1i, w1o = ffn1_w_in[0].astype(BF16), ffn1_w_out[0].astype(BF16)
    w2i, w2o = ffn2_w_in[0].astype(BF16), ffn2_w_out[0].astype(BF16)
    wmix, a2, ab = _mix_weights(w_mix_in[0], gla_a2_fwd[0], gla_a2_bwd[0], gla_a_bias_fwd[0], gla_a_bias_bwd[0])
    lng, lnb = ln_gain[0], ln_bias[0]
    hgain = jnp.concatenate([jnp.tile(gla_norm_gain[0], GLA_HEADS), jnp.tile(hgrn_norm_gain[0], HGRN_HEADS)])[None, :]

    ktf_c, ktb_c, v_c, dlf_c, dlb_c = _ffn_proj_call(False, ctx, None, mod, bsz, lng, lnb, w1i, w1o, wmix, a2, ab,
                                                     hgrn_lb_logits, None, tmc)
    sf0, sb0 = _ctx_state_call(ktf_c, ktb_c, v_c, dlf_c, dlb_c)
    x1, ff, fb, v, g, dlf, dlb, sb = _ffn_proj_call(True, x, pos, mod, bsz, lng, lnb, w1i, w1o, wmix, a2, ab,
                                                    hgrn_lb_logits, sb0, tm)
    o = _scan_call(ff, fb, v, dlf, dlb, sb, sf0, tm)
    return _out_ffn_call(x1, o, g, mod, lng, lnb, hgain, w_mix_out[0].astype(BF16), w2i, w2o, min(2 * SUB_TILE, t))
```

```python
import functools

import numpy as np
import jax
import jax.numpy as jnp
from jax import lax
from jax.experimental import pallas as pl
from jax.experimental.pallas import tpu as pltpu

N_SUBLAYERS = 3
N_MOD = 3 * N_SUBLAYERS
FFN_HALF = 0.5
GLA_HEADS = 4
GLA_DK = 64
GLA_DV = 128
GLA_GATE_RANK = 16
GLA_GATE_NORMALIZER = 16.0
HGRN_HEADS = 4
HGRN_DK = 128
HGRN_DV = 128
GLA_QK = GLA_HEADS * GLA_DK
GLA_V = GLA_HEADS * GLA_DV
HGRN_K = HGRN_HEADS * HGRN_DK
HGRN_V = HGRN_HEADS * HGRN_DV
CHUNK = 64
HALF = CHUNK // 2
CHUNK_SHIFT = 6
LN_EPS = 1e-5
NORM_EPS = 1e-6
POS_THETA = 10000.0
GRID_W = 64
DEPTH = 1
DN_ALPHA = (2.0 * DEPTH) ** 0.25

LANES = 128
SUBLANES = 8
FF_CHUNK = 256
SUB_TILE = 256
ROW_PART = 128
DECAY_W = GLA_QK + HGRN_K
N_GROUPS = GLA_HEADS // 2 + HGRN_HEADS
N_HEADS = GLA_HEADS + HGRN_HEADS
VMEM_LIMIT = 56 * 1024 * 1024

_C_GQ, _C_GK, _C_GV, _C_GG = 0, 256, 512, 1024
_C_RQ, _C_RFF, _C_RFB, _C_RI, _C_RG, _C_LR = 1536, 2048, 2560, 3072, 3584, 4096
MIX_W = 4096 + LANES

F32 = jnp.float32
BF16 = jnp.bfloat16


def _dot(a, b):
    return jnp.dot(a, b, preferred_element_type=F32)


def _dot_nt(a, b):
    return lax.dot_general(a, b, (((1,), (1,)), ((), ())), preferred_element_type=F32)


def _dot_tn(a, b):
    return lax.dot_general(a, b, (((0,), (0,)), ((), ())), preferred_element_type=F32)


def _sigmoid(x):
    return 1.0 / (1.0 + jnp.exp(-x))


def _silu(x):
    return x * _sigmoid(x)


def _ln(x):
    mu = jnp.mean(x, axis=-1, keepdims=True)
    xc = x - mu
    var = jnp.mean(xc * xc, axis=-1, keepdims=True)
    return xc * lax.rsqrt(var + LN_EPS)


def _post_norm(x, y, gate, weight, gain, bias):
    return _ln(DN_ALPHA * x + (weight * gate) * y) * gain + bias


def _ada_kernel(c_ref, w_ref, b_ref, o_ref):
    s = _silu(c_ref[...]).astype(BF16)
    o_ref[...] = _dot(s, w_ref[...].astype(BF16)) + b_ref[...]


def _ada_call(cc, w, b):
    rows, d = cc.shape
    n = w.shape[1]
    tn = 1536
    return pl.pallas_call(
        _ada_kernel,
        grid=(n // tn,),
        in_specs=[pl.BlockSpec((rows, d), lambda j: (0, 0)),
                  pl.BlockSpec((d, tn), lambda j: (0, j)),
                  pl.BlockSpec((1, tn), lambda j: (0, j))],
        out_specs=pl.BlockSpec((rows, tn), lambda j: (0, j)),
        out_shape=jax.ShapeDtypeStruct((rows, n), F32),
        compiler_params=pltpu.CompilerParams(dimension_semantics=("arbitrary",), vmem_limit_bytes=VMEM_LIMIT),
        name="ada",
    )(cc, w, b)


def _lane_masks():
    lane = lax.broadcasted_iota(jnp.int32, (1, LANES), 1)
    return lane < GLA_DK, lane >= GLA_DK


def _stack_pair(a, m0, m1):
    zero = jnp.zeros_like(a)
    return jnp.concatenate([jnp.where(m0, a, zero), jnp.where(m1, a, zero)], axis=0)


def _state_update(st_ref, dl, grp, kt, v):
    m0, m1 = _lane_masks()
    ktg = kt[:, grp * LANES:(grp + 1) * LANES]
    dlg = dl[:, grp * LANES:(grp + 1) * LANES]
    if grp < GLA_HEADS // 2:
        h0 = 2 * grp
        vs = jnp.concatenate([v[:, h0 * GLA_DV:(h0 + 1) * GLA_DV], v[:, (h0 + 1) * GLA_DV:(h0 + 2) * GLA_DV]], axis=0)
        ks = _stack_pair(ktg, m0, m1)
    else:
        h = grp - GLA_HEADS // 2
        vs = v[:, GLA_V + h * HGRN_DV:GLA_V + (h + 1) * HGRN_DV]
        ks = ktg
    st_ref[grp] = dlg * st_ref[grp] + _dot_tn(vs, ks)


def _seg_cumsum(x, reverse):
    row = lax.broadcasted_iota(jnp.int32, (SUBLANES, 1), 0)
    groups = [x[r:r + SUBLANES] for r in range(0, CHUNK, SUBLANES)]
    out = []
    carry = None
    for g in (reversed(groups) if reverse else groups):
        s = 1
        while s < SUBLANES:
            if reverse:
                g = g + jnp.where(row < SUBLANES - s, pltpu.roll(g, SUBLANES - s, 0), 0.0)
            else:
                g = g + jnp.where(row >= s, pltpu.roll(g, s, 0), 0.0)
            s *= 2
        if carry is not None:
            g = g + carry
        carry = g[0:1] if reverse else g[SUBLANES - 1:SUBLANES]
        out.append(g)
    return jnp.concatenate(out[::-1] if reverse else out, axis=0)


def _ffn_proj_kernel(latent, *refs):
    if latent:
        (x_ref, pos_ref, mod_ref, lng_ref, lnb_ref, w1i_ref, w1o_ref, wmix_ref, a2_ref, ab_ref, lbl_ref, sb0_ref,
         x1_ref, ff_ref, fb_ref, v_ref, g_ref, dlf_ref, dlb_ref, sbo_ref, a_scr, kb_scr, sb_scr) = refs
    else:
        (x_ref, mod_ref, lng_ref, lnb_ref, w1i_ref, w1o_ref, wmix_ref, a2_ref, ab_ref, lbl_ref,
         ff_ref, fb_ref, v_ref, dlf_ref, dlb_ref, a_scr) = refs
    tm = x_ref.shape[1]
    cpt = tm // CHUNK
    d_ff = w1o_ref.shape[0]
    m = mod_ref[0]
    x = x_ref[0]
    if latent:
        x = x + pos_ref[...]
    h = (_ln(x) * (1.0 + m[0:1]) + m[1:2]).astype(BF16)
    for c0 in range(0, d_ff, FF_CHUNK):
        g = _dot(h, w1i_ref[:, c0:c0 + FF_CHUNK])
        u = _dot(h, w1i_ref[:, d_ff + c0:d_ff + c0 + FF_CHUNK])
        a_scr[:, c0:c0 + FF_CHUNK] = (_silu(g) * u).astype(BF16)

    rp = min(ROW_PART, tm)
    parts = [slice(r, r + rp) for r in range(0, tm, rp)]
    ys = [_dot(a_scr[rows, :], w1o_ref[...]) for rows in parts]

    staged = []
    for rows, y in zip(parts, ys):
        x1 = _post_norm(x[rows], y, m[2:3], FFN_HALF, lng_ref[0:1], lnb_ref[0:1])
        if latent:
            x1_ref[0, rows] = x1
        h2 = (_ln(x1) * (1.0 + m[3:4]) + m[4:5]).astype(BF16)

        def proj(c0, width, h2=h2):
            return _dot(h2, wmix_ref[:, c0:c0 + width])

        st = {}
        lr = proj(_C_LR, LANES).astype(BF16)
        st["z"] = _dot(lr, a2_ref[...]) + ab_ref[...]
        st["k"] = proj(_C_GK, GLA_QK)
        st["rf"] = (proj(_C_RFF, HGRN_K), proj(_C_RFB, HGRN_K))
        v_ref[0, rows, 0:GLA_V] = proj(_C_GV, GLA_V).astype(BF16)
        v_ref[0, rows, GLA_V:GLA_V + HGRN_V] = proj(_C_RI, HGRN_V).astype(BF16)
        if latent:
            st["q"] = proj(_C_GQ, GLA_QK) * (GLA_DK ** -0.5)
            st["rq"] = proj(_C_RQ, HGRN_K)
            g_ref[0, rows, 0:GLA_V] = proj(_C_GG, GLA_V).astype(BF16)
            g_ref[0, rows, GLA_V:GLA_V + HGRN_V] = proj(_C_RG, HGRN_V).astype(BF16)
        staged.append(st)

    lbl = lbl_ref[...]
    run = None
    for part, st in zip(parts, staged):
        z = st["z"]
        g_gla = (jnp.minimum(z, 0.0) - jnp.log(1.0 + jnp.exp(-jnp.abs(z)))) * (1.0 / GLA_GATE_NORMALIZER)
        if latent:
            q_h = _silu(st["rq"]) * (HGRN_DK ** -0.5)
        for dirn, (f_ref, dl_ref) in enumerate(((ff_ref, dlf_ref), (fb_ref, dlb_ref))):
            levels = [lbl[dirn, i:i + 1, :] for i in range(lbl.shape[1])]
            mx = functools.reduce(jnp.maximum, levels)
            es = [jnp.exp(r - mx) for r in levels]
            lb = es[0] / functools.reduce(lambda a, b: a + b, es)
            f = lb + (1.0 - lb) * _sigmoid(st["rf"][dirn])
            logf = jnp.log(f)
            kk = 1.0 - f
            g_d = g_gla[:, dirn * GLA_QK:(dirn + 1) * GLA_QK]
            for cl in range(rp // CHUNK):
                c = part.start // CHUNK + cl
                l0, l1 = cl * CHUNK, (cl + 1) * CHUNK
                r0, r1 = c * CHUNK, (c + 1) * CHUNK
                b = _seg_cumsum(jnp.concatenate([g_d[l0:l1], logf[l0:l1]], axis=1), reverse=(dirn == 1))
                bl = b[0:1] if dirn == 1 else b[CHUNK - 1:CHUNK]
                bm = b[HALF:HALF + 1] if dirn == 1 else b[HALF - 1:HALF]
                kcat = jnp.concatenate([st["k"][l0:l1], kk[l0:l1]], axis=1)
                ktail = kcat * jnp.exp(bl - b)
                dl = jnp.exp(bl)
                if latent:
                    qcat = jnp.concatenate([st["q"][l0:l1], q_h[l0:l1]], axis=1)
                    f_ref[0, r0:r1, 0:DECAY_W] = (qcat * jnp.exp(b - bm)).astype(BF16)
                    f_ref[0, r0:r1, DECAY_W:2 * DECAY_W] = (kcat * jnp.exp(bm - b)).astype(BF16)
                    f_ref[0, r0:r1, 2 * DECAY_W:3 * DECAY_W] = ktail.astype(BF16)
                    if dirn == 1:
                        kb_scr[r0:r1, :] = (ktail if run is None else ktail * run).astype(BF16)
                        run = dl if run is None else run * dl
                else:
                    f_ref[0, r0:r1, :] = ktail.astype(BF16)
                dl_ref[0, c:c + 1, :] = dl
                dl_ref[0, cpt + c:cpt + c + 1, :] = jnp.exp(bm)

    if latent:
        @pl.when(pl.program_id(1) == 0)
        def _():
            sb_scr[...] = sb0_ref[0]

        sbo_ref[0, 0] = sb_scr[...]
        kb = kb_scr[...]
        vv = v_ref[0]
        for grp in range(N_GROUPS):
            _state_update(sb_scr, run, grp, kb, vv)


def _const_spec(shape):
    nd = len(shape)
    return pl.BlockSpec(shape, lambda *_: (0,) * nd, pipeline_mode=pl.Buffered(1))


def _ffn_proj_call(latent, x, pos, mod, mod_row0, lng, lnb, w1i, w1o, wmix, a2, ab, lbl, sb0, tm):
    bsz, t, d = x.shape
    nt = t // tm
    d_ff = w1o.shape[0]
    cpt = tm // CHUNK
    tile = (lambda b, i: (b, nt - 1 - i, 0)) if latent else (lambda b, i: (b, i, 0))
    tok = lambda w: pl.BlockSpec((1, tm, w), tile)
    in_specs = [tok(d)]
    args = [x]
    if latent:
        in_specs.append(pl.BlockSpec((tm, d), lambda b, i: (nt - 1 - i, 0)))
        args.append(pos)
        mod_map = lambda b, i: (b, 0, 0)
    else:
        mod_map = lambda b, i: (mod_row0, 0, 0)
    in_specs += [pl.BlockSpec((1, N_MOD, d), mod_map),
                 _const_spec(lng.shape), _const_spec(lnb.shape), _const_spec(w1i.shape), _const_spec(w1o.shape),
                 _const_spec(wmix.shape), _const_spec(a2.shape), _const_spec(ab.shape), _const_spec(lbl.shape)]
    args += [mod, lng, lnb, w1i, w1o, wmix, a2, ab, lbl]
    fw = 3 * DECAY_W if latent else DECAY_W
    dl_spec = pl.BlockSpec((1, 2 * cpt, DECAY_W), tile)
    dl_shape = jax.ShapeDtypeStruct((bsz, 2 * t // CHUNK, DECAY_W), F32)
    f_shape = jax.ShapeDtypeStruct((bsz, t, fw), BF16)
    v_shape = jax.ShapeDtypeStruct((bsz, t, GLA_V + HGRN_V), BF16)
    scratch = [pltpu.VMEM((tm, d_ff), BF16)]
    if latent:
        s_block = (1, N_GROUPS, LANES, LANES)
        in_specs.append(pl.BlockSpec(s_block, lambda b, i: (b, 0, 0, 0)))
        args.append(sb0)
        out_specs = [tok(d), tok(fw), tok(fw), tok(GLA_V + HGRN_V), tok(GLA_V + HGRN_V), dl_spec, dl_spec,
                     pl.BlockSpec((1,) + s_block, lambda b, i: (b, nt - 1 - i, 0, 0, 0))]
        out_shape = [jax.ShapeDtypeStruct((bsz, t, d), F32), f_shape, f_shape, v_shape, v_shape, dl_shape, dl_shape,
                     jax.ShapeDtypeStruct((bsz, nt) + s_block[1:], F32)]
        scratch += [pltpu.VMEM((tm, DECAY_W), BF16), pltpu.VMEM(s_block[1:], F32)]
    else:
        out_specs = [tok(fw), tok(fw), tok(GLA_V + HGRN_V), dl_spec, dl_spec]
        out_shape = [f_shape, f_shape, v_shape, dl_shape, dl_shape]
    return pl.pallas_call(
        functools.partial(_ffn_proj_kernel, latent),
        grid=(bsz, nt),
        in_specs=in_specs,
        out_specs=out_specs,
        out_shape=out_shape,
        scratch_shapes=scratch,
        compiler_params=pltpu.CompilerParams(dimension_semantics=("arbitrary", "arbitrary"),
                                             vmem_limit_bytes=VMEM_LIMIT),
        name="ffn_proj_latent" if latent else "ffn_proj_ctx",
    )(*args)


def _ctx_state_kernel(ktf_ref, ktb_ref, v_ref, dlf_ref, dlb_ref, sf_ref, sb_ref):
    n_chunks = v_ref.shape[1] // CHUNK
    sf_ref[...] = jnp.zeros_like(sf_ref)
    sb_ref[...] = jnp.zeros_like(sb_ref)
    for i in range(n_chunks):
        for kt_ref, dl_ref, s_ref, c in ((ktf_ref, dlf_ref, sf_ref, i), (ktb_ref, dlb_ref, sb_ref, n_chunks - 1 - i)):
            kt = kt_ref[0, c * CHUNK:(c + 1) * CHUNK, :]
            v = v_ref[0, c * CHUNK:(c + 1) * CHUNK, :]
            dl = dl_ref[0, c:c + 1, :]
            for grp in range(N_GROUPS):
                _state_update(s_ref.at[0], dl, grp, kt, v)


def _ctx_state_call(ktf, ktb, v, dlf, dlb):
    bsz, tc, _ = v.shape
    full = lambda a: pl.BlockSpec((1,) + a.shape[1:], lambda b: (b,) + (0,) * (a.ndim - 1))
    s_shape = jax.ShapeDtypeStruct((bsz, N_GROUPS, LANES, LANES), F32)
    s_spec = pl.BlockSpec((1, N_GROUPS, LANES, LANES), lambda b: (b, 0, 0, 0))
    return pl.pallas_call(
        _ctx_state_kernel,
        grid=(bsz,),
        in_specs=[full(ktf), full(ktb), full(v), full(dlf), full(dlb)],
        out_specs=[s_spec, s_spec],
        out_shape=[s_shape, s_shape],
        compiler_params=pltpu.CompilerParams(dimension_semantics=("arbitrary",), vmem_limit_bytes=VMEM_LIMIT),
        name="ctx_state",
    )(ktf, ktb, v, dlf, dlb)


def _scan_kernel(ff_ref, fb_ref, v_ref, dlf_ref, dlb_ref, sb_ref, sf0_ref, o_ref, sf_scr):
    tb = v_ref.shape[1]
    nc = tb // CHUNK

    @pl.when(pl.program_id(1) == 0)
    def _():
        for h in range(N_HEADS):
            sf_scr[h] = sf0_ref[0, h // 2 if h < GLA_HEADS else h - GLA_HEADS // 2]

    ii = lax.broadcasted_iota(jnp.int32, (tb, tb), 0)
    jj = lax.broadcasted_iota(jnp.int32, (tb, tb), 1)
    same = (ii >> CHUNK_SHIFT) == (jj >> CHUNK_SHIFT)
    mask_f = same & (jj <= ii)
    mask_b = same & (jj >= ii)
    tok_chunk = lax.broadcasted_iota(jnp.int32, (1, tb), 1) >> CHUNK_SHIFT
    m_pair = _lane_masks()

    for h in range(N_HEADS):
        grp = h // 2 if h < GLA_HEADS else h - GLA_HEADS // 2
        lo = grp * LANES
        vcol = h * LANES
        qf = ff_ref[0, :, lo:lo + LANES]
        qb = fb_ref[0, :, lo:lo + LANES]
        if h < GLA_HEADS:
            qf = jnp.where(m_pair[h % 2], qf, jnp.zeros_like(qf))
            qb = jnp.where(m_pair[h % 2], qb, jnp.zeros_like(qb))
        kif = ff_ref[0, :, DECAY_W + lo:DECAY_W + lo + LANES]
        kib = fb_ref[0, :, DECAY_W + lo:DECAY_W + lo + LANES]
        ktf = ff_ref[0, :, 2 * DECAY_W + lo:2 * DECAY_W + lo + LANES]
        ktb = fb_ref[0, :, 2 * DECAY_W + lo:2 * DECAY_W + lo + LANES]
        vh = v_ref[0, :, vcol:vcol + LANES]

        vt = vh.T
        zero = jnp.zeros_like(vt)
        lhs = jnp.concatenate([jnp.where(tok_chunk == c, vt, zero) for c in range(nc)], axis=0)
        u = _dot(lhs, jnp.concatenate([ktf, ktb], axis=1))

        st_f = [None] * nc
        cur = sf_scr[h]
        for c in range(nc):
            st_f[c] = (dlf_ref[0, nc + c:nc + c + 1, lo:lo + LANES] * cur).astype(BF16)
            cur = dlf_ref[0, c:c + 1, lo:lo + LANES] * cur + u[c * LANES:(c + 1) * LANES, 0:LANES]
        sf_scr[h] = cur
        st_b = [None] * nc
        cur = sb_ref[0, 0, grp]
        for c in reversed(range(nc)):
            st_b[c] = (dlb_ref[0, nc + c:nc + c + 1, lo:lo + LANES] * cur).astype(BF16)
            cur = dlb_ref[0, c:c + 1, lo:lo + LANES] * cur + u[c * LANES:(c + 1) * LANES, LANES:2 * LANES]

        att = jnp.where(mask_f, _dot_nt(qf, kif), 0.0) + jnp.where(mask_b, _dot_nt(qb, kib), 0.0)
        o = _dot(att.astype(BF16), vh)
        for c in range(nc):
            r0, r1 = c * CHUNK, (c + 1) * CHUNK
            qc = jnp.concatenate([qf[r0:r1], qb[r0:r1]], axis=1)
            sc = jnp.concatenate([st_f[c], st_b[c]], axis=1)
            o_ref[0, r0:r1, vcol:vcol + LANES] = o[r0:r1] + _dot_nt(qc, sc)


def _scan_call(ff, fb, v, dlf, dlb, sb, sf0, tb):
    bsz, t, _ = v.shape
    nt = t // tb
    cpt = tb // CHUNK
    tok = lambda w: pl.BlockSpec((1, tb, w), lambda b, i: (b, i, 0))
    s_block = (1, N_GROUPS, LANES, LANES)
    return pl.pallas_call(
        _scan_kernel,
        grid=(bsz, nt),
        in_specs=[tok(ff.shape[2]), tok(fb.shape[2]), tok(v.shape[2]),
                  pl.BlockSpec((1, 2 * cpt, DECAY_W), lambda b, i: (b, i, 0)),
                  pl.BlockSpec((1, 2 * cpt, DECAY_W), lambda b, i: (b, i, 0)),
                  pl.BlockSpec((1,) + s_block, lambda b, i: (b, i, 0, 0, 0)),
                  pl.BlockSpec(s_block, lambda b, i: (b, 0, 0, 0))],
        out_specs=tok(GLA_V + HGRN_V),
        out_shape=jax.ShapeDtypeStruct((bsz, t, GLA_V + HGRN_V), F32),
        scratch_shapes=[pltpu.VMEM((N_HEADS, LANES, LANES), F32)],
        compiler_params=pltpu.CompilerParams(dimension_semantics=("arbitrary", "arbitrary"),
                                             vmem_limit_bytes=VMEM_LIMIT),
        name="scan",
    )(ff, fb, v, dlf, dlb, sb, sf0)


def _out_ffn_kernel(x1_ref, o_ref, g_ref, mod_ref, lng_ref, lnb_ref, hg_ref, wo_ref, w2i_ref, w2o_ref,
                    out_ref, a_scr, mg_scr, y_scr):
    m = mod_ref[0]
    d_ff = w2o_ref.shape[0]
    subs = [slice(r0, r0 + SUB_TILE) for r0 in range(0, x1_ref.shape[1], SUB_TILE)]

    def prologue(rows):
        for h in range(N_HEADS):
            lanes = slice(h * LANES, (h + 1) * LANES)
            oh = o_ref[0, rows, lanes]
            ms = jnp.mean(oh * oh, axis=-1, keepdims=True)
            gate = g_ref[0, rows, lanes].astype(F32)
            mg_scr[:, lanes] = (oh * lax.rsqrt(ms + NORM_EPS) * hg_ref[:, lanes] * _silu(gate)).astype(BF16)
        y = _dot(mg_scr[...], wo_ref[...])
        x2 = _post_norm(x1_ref[0, rows], y, m[5:6], 1.0, lng_ref[1:2], lnb_ref[1:2])
        return x2, (_ln(x2) * (1.0 + m[6:7]) + m[7:8]).astype(BF16)

    def hidden(h3):
        for c0 in range(0, d_ff, FF_CHUNK):
            g = _dot(h3, w2i_ref[:, c0:c0 + FF_CHUNK])
            u = _dot(h3, w2i_ref[:, d_ff + c0:d_ff + c0 + FF_CHUNK])
            a_scr[:, c0:c0 + FF_CHUNK] = (_silu(g) * u).astype(BF16)

    x2, h3 = prologue(subs[0])
    for k, rows in enumerate(subs):
        hidden(h3)
        if k + 1 < len(subs):
            x2_next, h3_next = prologue(subs[k + 1])
        y_scr[...] = _dot(a_scr[...], w2o_ref[...])
        out_ref[0, rows] = _post_norm(x2, y_scr[...], m[8:9], FFN_HALF, lng_ref[2:3], lnb_ref[2:3])
        if k + 1 < len(subs):
            x2, h3 = x2_next, h3_next


def _out_ffn_call(x1, o, g, mod, lng, lnb, hgain, wo, w2i, w2o, tm):
    bsz, t, d = x1.shape
    nt = t // tm
    d_ff = w2o.shape[0]
    tok = lambda w: pl.BlockSpec((1, tm, w), lambda b, i: (b, i, 0))
    return pl.pallas_call(
        _out_ffn_kernel,
        grid=(bsz, nt),
        in_specs=[tok(d), tok(o.shape[2]), tok(g.shape[2]),
                  pl.BlockSpec((1, N_MOD, d), lambda b, i: (b, 0, 0)),
                  _const_spec(lng.shape), _const_spec(lnb.shape), _const_spec(hgain.shape),
                  _const_spec(wo.shape), _const_spec(w2i.shape), _const_spec(w2o.shape)],
        out_specs=tok(d),
        out_shape=jax.ShapeDtypeStruct((bsz, t, d), F32),
        scratch_shapes=[pltpu.VMEM((SUB_TILE, d_ff), BF16), pltpu.VMEM((SUB_TILE, o.shape[2]), BF16),
                        pltpu.VMEM((SUB_TILE, d), F32)],
        compiler_params=pltpu.CompilerParams(dimension_semantics=("arbitrary", "arbitrary"),
                                             vmem_limit_bytes=VMEM_LIMIT),
        name="out_ffn",
    )(x1, o, g, mod, lng, lnb, hgain, wo, w2i, w2o)


def _sincos_2d(rows, width, dim):
    r = jnp.repeat(jnp.arange(rows), width)
    col = jnp.tile(jnp.arange(width), rows)
    quarter = dim // 4
    omega = 1.0 / POS_THETA ** (jnp.arange(quarter, dtype=F32) / quarter)

    def emb(p):
        a = p.astype(F32)[:, None] * omega[None, :]
        return jnp.concatenate([jnp.sin(a), jnp.cos(a)], axis=-1)

    return jnp.concatenate([emb(r), emb(col)], axis=-1)


def _mix_weights(w_mix_in, a2_f, a2_b, ab_f, ab_b):
    d = w_mix_in.shape[0]
    o = np.cumsum((0, GLA_QK, GLA_QK, GLA_V, GLA_V, GLA_GATE_RANK, GLA_GATE_RANK, HGRN_K, HGRN_K, HGRN_K, HGRN_V, HGRN_V))
    seg = lambda i: w_mix_in[:, o[i]:o[i + 1]]
    pad = jnp.zeros((d, LANES - 2 * GLA_GATE_RANK), w_mix_in.dtype)
    wmix = jnp.concatenate([seg(0), seg(1), seg(2), seg(3), seg(6), seg(7), seg(8), seg(9), seg(10), seg(4), seg(5), pad],
                           axis=1).astype(BF16)
    a2 = jnp.zeros((LANES, 2 * GLA_QK), F32)
    a2 = a2.at[0:GLA_GATE_RANK, 0:GLA_QK].set(a2_f).at[GLA_GATE_RANK:2 * GLA_GATE_RANK, GLA_QK:].set(a2_b)
    ab = jnp.concatenate([ab_f, ab_b])[None, :]
    return wmix, a2.astype(BF16), ab


def kernel(x, c, ctx, c_ctx, w_ada, b_ada, ln_gain, ln_bias, ffn1_w_in, ffn1_w_out, w_mix_in, gla_a2_fwd, gla_a2_bwd,
           gla_a_bias_fwd, gla_a_bias_bwd, hgrn_lb_logits, gla_norm_gain, hgrn_norm_gain, w_mix_out, ffn2_w_in,
           ffn2_w_out):
    bsz, t, d = x.shape
    tm = min(256, t)
    tmc = min(256, ctx.shape[1])
    pos = _sincos_2d(t // GRID_W, GRID_W, d).astype(x.dtype)

    mod_rows = -(-(bsz + 1) // 8) * 8
    cc = jnp.zeros((mod_rows, d), F32).at[:bsz].set(c).at[bsz].set(c_ctx)
    mod = _ada_call(cc, w_ada[0], b_ada[0][None, :]).reshape(mod_rows, N_MOD, d)

    w1i, w1o = ffn1_w_in[0].astype(BF16), ffn1_w_out[0].astype(BF16)
    w2i, w2o = ffn2_w_in[0].astype(BF16), ffn2_w_out[0].astype(BF16)
    wmix, a2, ab = _mix_weights(w_mix_in[0], gla_a2_fwd[0], gla_a2_bwd[0], gla_a_bias_fwd[0], gla_a_bias_bwd[0])
    lng, lnb = ln_gain[0], ln_bias[0]
    hgain = jnp.concatenate([jnp.tile(gla_norm_gain[0], GLA_HEADS), jnp.tile(hgrn_norm_gain[0], HGRN_HEADS)])[None, :]

    ktf_c, ktb_c, v_c, dlf_c, dlb_c = _ffn_proj_call(False, ctx, None, mod, bsz, lng, lnb, w1i, w1o, wmix, a2, ab,
                                                     hgrn_lb_logits, None, tmc)
    sf0, sb0 = _ctx_state_call(ktf_c, ktb_c, v_c, dlf_c, dlb_c)
    x1, ff, fb, v, g, dlf, dlb, sb = _ffn_proj_call(True, x, pos, mod, bsz, lng, lnb, w1i, w1o, wmix, a2, ab,
                                                    hgrn_lb_logits, sb0, tm)
    o = _scan_call(ff, fb, v, dlf, dlb, sb, sf0, tm)
    return _out_ffn_call(x1, o, g, mod, lng, lnb, hgain, w_mix_out[0].astype(BF16), w2i, w2o, min(2 * SUB_TILE, t))
```

```python
import functools

import numpy as np
import jax
import jax.numpy as jnp
from jax import lax
from jax.experimental import pallas as pl
from jax.experimental.pallas import tpu as pltpu

N_SUBLAYERS = 3
N_MOD = 3 * N_SUBLAYERS
FFN_HALF = 0.5
GLA_HEADS = 4
GLA_DK = 64
GLA_DV = 128
GLA_GATE_RANK = 16
GLA_GATE_NORMALIZER = 16.0
HGRN_HEADS = 4
HGRN_DK = 128
HGRN_DV = 128
GLA_QK = GLA_HEADS * GLA_DK
GLA_V = GLA_HEADS * GLA_DV
HGRN_K = HGRN_HEADS * HGRN_DK
HGRN_V = HGRN_HEADS * HGRN_DV
CHUNK = 64
HALF = CHUNK // 2
CHUNK_SHIFT = 6
LN_EPS = 1e-5
NORM_EPS = 1e-6
POS_THETA = 10000.0
GRID_W = 64
DEPTH = 1
DN_ALPHA = (2.0 * DEPTH) ** 0.25

LANES = 128
SUBLANES = 8
FF_CHUNK = 256
SUB_TILE = 256
EARLY_HIDDEN = 3
ROW_PART = 128
DECAY_W = GLA_QK + HGRN_K
N_GROUPS = GLA_HEADS // 2 + HGRN_HEADS
N_HEADS = GLA_HEADS + HGRN_HEADS
VMEM_LIMIT = 56 * 1024 * 1024

_C_GQ, _C_GK, _C_GV, _C_GG = 0, 256, 512, 1024
_C_RQ, _C_RFF, _C_RFB, _C_RI, _C_RG, _C_LR = 1536, 2048, 2560, 3072, 3584, 4096
MIX_W = 4096 + LANES

F32 = jnp.float32
BF16 = jnp.bfloat16


def _dot(a, b):
    return jnp.dot(a, b, preferred_element_type=F32)


def _dot_nt(a, b):
    return lax.dot_general(a, b, (((1,), (1,)), ((), ())), preferred_element_type=F32)


def _dot_tn(a, b):
    return lax.dot_general(a, b, (((0,), (0,)), ((), ())), preferred_element_type=F32)


def _sigmoid(x):
    return 1.0 / (1.0 + jnp.exp(-x))


def _silu(x):
    return x * _sigmoid(x)


def _ln(x):
    mu = jnp.mean(x, axis=-1, keepdims=True)
    xc = x - mu
    var = jnp.mean(xc * xc, axis=-1, keepdims=True)
    return xc * lax.rsqrt(var + LN_EPS)


def _post_norm(x, y, gate, weight, gain, bias):
    return _ln(DN_ALPHA * x + (weight * gate) * y) * gain + bias


def _ada_kernel(c_ref, w_ref, b_ref, o_ref):
    s = _silu(c_ref[...]).astype(BF16)
    o_ref[...] = _dot(s, w_ref[...].astype(BF16)) + b_ref[...]


def _ada_call(cc, w, b):
    rows, d = cc.shape
    n = w.shape[1]
    tn = 1536
    return pl.pallas_call(
        _ada_kernel,
        grid=(n // tn,),
        in_specs=[pl.BlockSpec((rows, d), lambda j: (0, 0)),
                  pl.BlockSpec((d, tn), lambda j: (0, j)),
                  pl.BlockSpec((1, tn), lambda j: (0, j))],
        out_specs=pl.BlockSpec((rows, tn), lambda j: (0, j)),
        out_shape=jax.ShapeDtypeStruct((rows, n), F32),
        compiler_params=pltpu.CompilerParams(dimension_semantics=("arbitrary",), vmem_limit_bytes=VMEM_LIMIT),
        name="ada",
    )(cc, w, b)


def _lane_masks():
    lane = lax.broadcasted_iota(jnp.int32, (1, LANES), 1)
    return lane < GLA_DK, lane >= GLA_DK


def _stack_pair(a, m0, m1):
    zero = jnp.zeros_like(a)
    return jnp.concatenate([jnp.where(m0, a, zero), jnp.where(m1, a, zero)], axis=0)


def _state_update(st_ref, dl, grp, kt, v):
    m0, m1 = _lane_masks()
    ktg = kt[:, grp * LANES:(grp + 1) * LANES]
    dlg = dl[:, grp * LANES:(grp + 1) * LANES]
    if grp < GLA_HEADS // 2:
        h0 = 2 * grp
        vs = jnp.concatenate([v[:, h0 * GLA_DV:(h0 + 1) * GLA_DV], v[:, (h0 + 1) * GLA_DV:(h0 + 2) * GLA_DV]], axis=0)
        ks = _stack_pair(ktg, m0, m1)
    else:
        h = grp - GLA_HEADS // 2
        vs = v[:, GLA_V + h * HGRN_DV:GLA_V + (h + 1) * HGRN_DV]
        ks = ktg
    st_ref[grp] = dlg * st_ref[grp] + _dot_tn(vs, ks)


def _seg_cumsum(x, reverse):
    row = lax.broadcasted_iota(jnp.int32, (SUBLANES, 1), 0)
    groups = [x[r:r + SUBLANES] for r in range(0, CHUNK, SUBLANES)]
    out = []
    carry = None
    for g in (reversed(groups) if reverse else groups):
        s = 1
        while s < SUBLANES:
            if reverse:
                g = g + jnp.where(row < SUBLANES - s, pltpu.roll(g, SUBLANES - s, 0), 0.0)
            else:
                g = g + jnp.where(row >= s, pltpu.roll(g, s, 0), 0.0)
            s *= 2
        if carry is not None:
            g = g + carry
        carry = g[0:1] if reverse else g[SUBLANES - 1:SUBLANES]
        out.append(g)
    return jnp.concatenate(out[::-1] if reverse else out, axis=0)


def _ffn_proj_kernel(latent, *refs):
    if latent:
        (x_ref, pos_ref, mod_ref, lng_ref, lnb_ref, w1i_ref, w1o_ref, wmix_ref, a2_ref, ab_ref, lbl_ref, sb0_ref,
         x1_ref, ff_ref, fb_ref, v_ref, g_ref, dlf_ref, dlb_ref, sbo_ref, a_scr, kb_scr, sb_scr) = refs
    else:
        (x_ref, mod_ref, lng_ref, lnb_ref, w1i_ref, w1o_ref, wmix_ref, a2_ref, ab_ref, lbl_ref,
         ff_ref, fb_ref, v_ref, dlf_ref, dlb_ref, a_scr) = refs
    tm = x_ref.shape[1]
    cpt = tm // CHUNK
    d_ff = w1o_ref.shape[0]
    m = mod_ref[0]
    x = x_ref[0]
    if latent:
        x = x + pos_ref[...]
    rp = min(ROW_PART, tm)
    parts = [slice(r, r + rp) for r in range(0, tm, rp)]

    def hidden(hh, rows, c0):
        g = _dot(hh, w1i_ref[:, c0:c0 + FF_CHUNK])
        u = _dot(hh, w1i_ref[:, d_ff + c0:d_ff + c0 + FF_CHUNK])
        a_scr[rows, c0:c0 + FF_CHUNK] = (_silu(g) * u).astype(BF16)

    hs = []
    for rows in parts:
        hp = (_ln(x[rows]) * (1.0 + m[0:1]) + m[1:2]).astype(BF16)
        hs.append(hp)
        for c0 in range(0, EARLY_HIDDEN * FF_CHUNK, FF_CHUNK):
            hidden(hp, rows, c0)
    h = jnp.concatenate(hs, axis=0)
    for c0 in range(EARLY_HIDDEN * FF_CHUNK, d_ff, FF_CHUNK):
        hidden(h, slice(None), c0)

    ys = [_dot(a_scr[rows, :], w1o_ref[...]) for rows in parts]

    staged = []
    for rows, y in zip(parts, ys):
        x1 = _post_norm(x[rows], y, m[2:3], FFN_HALF, lng_ref[0:1], lnb_ref[0:1])
        if latent:
            x1_ref[0, rows] = x1
        h2 = (_ln(x1) * (1.0 + m[3:4]) + m[4:5]).astype(BF16)

        def proj(c0, width, h2=h2):
            return _dot(h2, wmix_ref[:, c0:c0 + width])

        st = {}
        lr = proj(_C_LR, LANES).astype(BF16)
        st["z"] = _dot(lr, a2_ref[...]) + ab_ref[...]
        st["k"] = proj(_C_GK, GLA_QK)
        st["rf"] = (proj(_C_RFF, HGRN_K), proj(_C_RFB, HGRN_K))
        v_ref[0, rows, 0:GLA_V] = proj(_C_GV, GLA_V).astype(BF16)
        v_ref[0, rows, GLA_V:GLA_V + HGRN_V] = proj(_C_RI, HGRN_V).astype(BF16)
        if latent:
            st["q"] = proj(_C_GQ, GLA_QK) * (GLA_DK ** -0.5)
            st["rq"] = proj(_C_RQ, HGRN_K)
            g_ref[0, rows, 0:GLA_V] = proj(_C_GG, GLA_V).astype(BF16)
            g_ref[0, rows, GLA_V:GLA_V + HGRN_V] = proj(_C_RG, HGRN_V).astype(BF16)
        staged.append(st)

    lbl = lbl_ref[...]
    run = None
    for part, st in zip(parts, staged):
        z = st["z"]
        g_gla = (jnp.minimum(z, 0.0) - jnp.log(1.0 + jnp.exp(-jnp.abs(z)))) * (1.0 / GLA_GATE_NORMALIZER)
        if latent:
            q_h = _silu(st["rq"]) * (HGRN_DK ** -0.5)
        for dirn, (f_ref, dl_ref) in enumerate(((ff_ref, dlf_ref), (fb_ref, dlb_ref))):
            levels = [lbl[dirn, i:i + 1, :] for i in range(lbl.shape[1])]
            mx = functools.reduce(jnp.maximum, levels)
            es = [jnp.exp(r - mx) for r in levels]
            lb = es[0] / functools.reduce(lambda a, b: a + b, es)
            f = lb + (1.0 - lb) * _sigmoid(st["rf"][dirn])
            logf = jnp.log(f)
            kk = 1.0 - f
            g_d = g_gla[:, dirn * GLA_QK:(dirn + 1) * GLA_QK]
            for cl in range(rp // CHUNK):
                c = part.start // CHUNK + cl
                l0, l1 = cl * CHUNK, (cl + 1) * CHUNK
                r0, r1 = c * CHUNK, (c + 1) * CHUNK
                b = _seg_cumsum(jnp.concatenate([g_d[l0:l1], logf[l0:l1]], axis=1), reverse=(dirn == 1))
                bl = b[0:1] if dirn == 1 else b[CHUNK - 1:CHUNK]
                bm = b[HALF:HALF + 1] if dirn == 1 else b[HALF - 1:HALF]
                kcat = jnp.concatenate([st["k"][l0:l1], kk[l0:l1]], axis=1)
                ktail = kcat * jnp.exp(bl - b)
                dl = jnp.exp(bl)
                if latent:
                    qcat = jnp.concatenate([st["q"][l0:l1], q_h[l0:l1]], axis=1)
                    f_ref[0, r0:r1, 0:DECAY_W] = (qcat * jnp.exp(b - bm)).astype(BF16)
                    f_ref[0, r0:r1, DECAY_W:2 * DECAY_W] = (kcat * jnp.exp(bm - b)).astype(BF16)
                    f_ref[0, r0:r1, 2 * DECAY_W:3 * DECAY_W] = ktail.astype(BF16)
                    if dirn == 1:
                        kb_scr[r0:r1, :] = (ktail if run is None else ktail * run).astype(BF16)
                        run = dl if run is None else run * dl
                else:
                    f_ref[0, r0:r1, :] = ktail.astype(BF16)
                dl_ref[0, c:c + 1, :] = dl
                dl_ref[0, cpt + c:cpt + c + 1, :] = jnp.exp(bm)

    if latent:
        @pl.when(pl.program_id(1) == 0)
        def _():
            sb_scr[...] = sb0_ref[0]

        sbo_ref[0, 0] = sb_scr[...]
        kb = kb_scr[...]
        vv = v_ref[0]
        for grp in range(N_GROUPS):
            _state_update(sb_scr, run, grp, kb, vv)


def _const_spec(shape):
    nd = len(shape)
    return pl.BlockSpec(shape, lambda *_: (0,) * nd, pipeline_mode=pl.Buffered(1))


def _ffn_proj_call(latent, x, pos, mod, mod_row0, lng, lnb, w1i, w1o, wmix, a2, ab, lbl, sb0, tm):
    bsz, t, d = x.shape
    nt = t // tm
    d_ff = w1o.shape[0]
    cpt = tm // CHUNK
    tile = (lambda b, i: (b, nt - 1 - i, 0)) if latent else (lambda b, i: (b, i, 0))
    tok = lambda w: pl.BlockSpec((1, tm, w), tile)
    in_specs = [tok(d)]
    args = [x]
    if latent:
        in_specs.append(pl.BlockSpec((tm, d), lambda b, i: (nt - 1 - i, 0)))
        args.append(pos)
        mod_map = lambda b, i: (b, 0, 0)
    else:
        mod_map = lambda b, i: (mod_row0, 0, 0)
    in_specs += [pl.BlockSpec((1, N_MOD, d), mod_map),
                 _const_spec(lng.shape), _const_spec(lnb.shape), _const_spec(w1i.shape), _const_spec(w1o.shape),
                 _const_spec(wmix.shape), _const_spec(a2.shape), _const_spec(ab.shape), _const_spec(lbl.shape)]
    args += [mod, lng, lnb, w1i, w1o, wmix, a2, ab, lbl]
    fw = 3 * DECAY_W if latent else DECAY_W
    dl_spec = pl.BlockSpec((1, 2 * cpt, DECAY_W), tile)
    dl_shape = jax.ShapeDtypeStruct((bsz, 2 * t // CHUNK, DECAY_W), F32)
    f_shape = jax.ShapeDtypeStruct((bsz, t, fw), BF16)
    v_shape = jax.ShapeDtypeStruct((bsz, t, GLA_V + HGRN_V), BF16)
    scratch = [pltpu.VMEM((tm, d_ff), BF16)]
    if latent:
        s_block = (1, N_GROUPS, LANES, LANES)
        in_specs.append(pl.BlockSpec(s_block, lambda b, i: (b, 0, 0, 0)))
        args.append(sb0)
        out_specs = [tok(d), tok(fw), tok(fw), tok(GLA_V + HGRN_V), tok(GLA_V + HGRN_V), dl_spec, dl_spec,
                     pl.BlockSpec((1,) + s_block, lambda b, i: (b, nt - 1 - i, 0, 0, 0))]
        out_shape = [jax.ShapeDtypeStruct((bsz, t, d), F32), f_shape, f_shape, v_shape, v_shape, dl_shape, dl_shape,
                     jax.ShapeDtypeStruct((bsz, nt) + s_block[1:], F32)]
        scratch += [pltpu.VMEM((tm, DECAY_W), BF16), pltpu.VMEM(s_block[1:], F32)]
    else:
        out_specs = [tok(fw), tok(fw), tok(GLA_V + HGRN_V), dl_spec, dl_spec]
        out_shape = [f_shape, f_shape, v_shape, dl_shape, dl_shape]
    return pl.pallas_call(
        functools.partial(_ffn_proj_kernel, latent),
        grid=(bsz, nt),
        in_specs=in_specs,
        out_specs=out_specs,
        out_shape=out_shape,
        scratch_shapes=scratch,
        compiler_params=pltpu.CompilerParams(dimension_semantics=("arbitrary", "arbitrary"),
                                             vmem_limit_bytes=VMEM_LIMIT),
        name="ffn_proj_latent" if latent else "ffn_proj_ctx",
    )(*args)


def _ctx_state_kernel(ktf_ref, ktb_ref, v_ref, dlf_ref, dlb_ref, sf_ref, sb_ref):
    n_chunks = v_ref.shape[1] // CHUNK
    sf_ref[...] = jnp.zeros_like(sf_ref)
    sb_ref[...] = jnp.zeros_like(sb_ref)
    for i in range(n_chunks):
        for kt_ref, dl_ref, s_ref, c in ((ktf_ref, dlf_ref, sf_ref, i), (ktb_ref, dlb_ref, sb_ref, n_chunks - 1 - i)):
            kt = kt_ref[0, c * CHUNK:(c + 1) * CHUNK, :]
            v = v_ref[0, c * CHUNK:(c + 1) * CHUNK, :]
            dl = dl_ref[0, c:c + 1, :]
            for grp in range(N_GROUPS):
                _state_update(s_ref.at[0], dl, grp, kt, v)


def _ctx_state_call(ktf, ktb, v, dlf, dlb):
    bsz, tc, _ = v.shape
    full = lambda a: pl.BlockSpec((1,) + a.shape[1:], lambda b: (b,) + (0,) * (a.ndim - 1))
    s_shape = jax.ShapeDtypeStruct((bsz, N_GROUPS, LANES, LANES), F32)
    s_spec = pl.BlockSpec((1, N_GROUPS, LANES, LANES), lambda b: (b, 0, 0, 0))
    return pl.pallas_call(
        _ctx_state_kernel,
        grid=(bsz,),
        in_specs=[full(ktf), full(ktb), full(v), full(dlf), full(dlb)],
        out_specs=[s_spec, s_spec],
        out_shape=[s_shape, s_shape],
        compiler_params=pltpu.CompilerParams(dimension_semantics=("arbitrary",), vmem_limit_bytes=VMEM_LIMIT),
        name="ctx_state",
    )(ktf, ktb, v, dlf, dlb)


def _scan_kernel(ff_all, fb_all, v_all, dlf_all, dlb_all, sb_all, sf0_ref, o_all, sf_scr):
    tb = SUB_TILE
    nc = tb // CHUNK

    @pl.when(pl.program_id(1) == 0)
    def _():
        for h in range(N_HEADS):
            sf_scr[h] = sf0_ref[0, h // 2 if h < GLA_HEADS else h - GLA_HEADS // 2]

    ii = lax.broadcasted_iota(jnp.int32, (tb, tb), 0)
    jj = lax.broadcasted_iota(jnp.int32, (tb, tb), 1)
    same = (ii >> CHUNK_SHIFT) == (jj >> CHUNK_SHIFT)
    mask_f = same & (jj <= ii)
    mask_b = same & (jj >= ii)
    tok_chunk = lax.broadcasted_iota(jnp.int32, (1, tb), 1) >> CHUNK_SHIFT
    m_pair = _lane_masks()

    n_sub = v_all.shape[1] // tb
    for sub, h in [(sub, h) for sub in range(n_sub) for h in range(N_HEADS)]:
        ff_ref, fb_ref, v_ref, o_ref = (r.at[:, sub * tb:(sub + 1) * tb] for r in (ff_all, fb_all, v_all, o_all))
        dlf_ref, dlb_ref = (r.at[:, sub * 2 * nc:(sub + 1) * 2 * nc] for r in (dlf_all, dlb_all))
        sb_ref = sb_all.at[:, sub:sub + 1]
        grp = h // 2 if h < GLA_HEADS else h - GLA_HEADS // 2
        lo = grp * LANES
        vcol = h * LANES
        qf = ff_ref[0, :, lo:lo + LANES]
        qb = fb_ref[0, :, lo:lo + LANES]
        if h < GLA_HEADS:
            qf = jnp.where(m_pair[h % 2], qf, jnp.zeros_like(qf))
            qb = jnp.where(m_pair[h % 2], qb, jnp.zeros_like(qb))
        kif = ff_ref[0, :, DECAY_W + lo:DECAY_W + lo + LANES]
        kib = fb_ref[0, :, DECAY_W + lo:DECAY_W + lo + LANES]
        ktf = ff_ref[0, :, 2 * DECAY_W + lo:2 * DECAY_W + lo + LANES]
        ktb = fb_ref[0, :, 2 * DECAY_W + lo:2 * DECAY_W + lo + LANES]
        vh = v_ref[0, :, vcol:vcol + LANES]

        vt = vh.T
        zero = jnp.zeros_like(vt)
        lhs = jnp.concatenate([jnp.where(tok_chunk == c, vt, zero) for c in range(nc)], axis=0)
        u = _dot(lhs, jnp.concatenate([ktf, ktb], axis=1))

        st_f = [None] * nc
        cur = sf_scr[h]
        for c in range(nc):
            st_f[c] = (dlf_ref[0, nc + c:nc + c + 1, lo:lo + LANES] * cur).astype(BF16)
            cur = dlf_ref[0, c:c + 1, lo:lo + LANES] * cur + u[c * LANES:(c + 1) * LANES, 0:LANES]
        sf_scr[h] = cur
        st_b = [None] * nc
        cur = sb_ref[0, 0, grp]
        for c in reversed(range(nc)):
            st_b[c] = (dlb_ref[0, nc + c:nc + c + 1, lo:lo + LANES] * cur).astype(BF16)
            cur = dlb_ref[0, c:c + 1, lo:lo + LANES] * cur + u[c * LANES:(c + 1) * LANES, LANES:2 * LANES]

        att = jnp.where(mask_f, _dot_nt(qf, kif), 0.0) + jnp.where(mask_b, _dot_nt(qb, kib), 0.0)
        o = _dot(att.astype(BF16), vh)
        for c in range(nc):
            r0, r1 = c * CHUNK, (c + 1) * CHUNK
            qc = jnp.concatenate([qf[r0:r1], qb[r0:r1]], axis=1)
            sc = jnp.concatenate([st_f[c], st_b[c]], axis=1)
            o_ref[0, r0:r1, vcol:vcol + LANES] = o[r0:r1] + _dot_nt(qc, sc)


def _scan_call(ff, fb, v, dlf, dlb, sb, sf0, tb):
    bsz, t, _ = v.shape
    nt = t // tb
    cpt = tb // CHUNK
    tok = lambda w: pl.BlockSpec((1, tb, w), lambda b, i: (b, i, 0))
    s_block = (1, N_GROUPS, LANES, LANES)
    n_sub = tb // SUB_TILE
    return pl.pallas_call(
        _scan_kernel,
        grid=(bsz, nt),
        in_specs=[tok(ff.shape[2]), tok(fb.shape[2]), tok(v.shape[2]),
                  pl.BlockSpec((1, 2 * cpt, DECAY_W), lambda b, i: (b, i, 0)),
                  pl.BlockSpec((1, 2 * cpt, DECAY_W), lambda b, i: (b, i, 0)),
                  pl.BlockSpec((1, n_sub) + s_block[1:], lambda b, i: (b, i, 0, 0, 0)),
                  pl.BlockSpec(s_block, lambda b, i: (b, 0, 0, 0))],
        out_specs=tok(GLA_V + HGRN_V),
        out_shape=jax.ShapeDtypeStruct((bsz, t, GLA_V + HGRN_V), F32),
        scratch_shapes=[pltpu.VMEM((N_HEADS, LANES, LANES), F32)],
        compiler_params=pltpu.CompilerParams(dimension_semantics=("arbitrary", "arbitrary"),
                                             vmem_limit_bytes=VMEM_LIMIT),
        name="scan",
    )(ff, fb, v, dlf, dlb, sb, sf0)


def _out_ffn_kernel(x1_ref, o_ref, g_ref, mod_ref, lng_ref, lnb_ref, hg_ref, wo_ref, w2i_ref, w2o_ref,
                    out_ref, a_scr, mg_scr, y_scr):
    m = mod_ref[0]
    d_ff = w2o_ref.shape[0]
    subs = [slice(r0, r0 + SUB_TILE) for r0 in range(0, x1_ref.shape[1], SUB_TILE)]

    def prologue(rows):
        for h in range(N_HEADS):
            lanes = slice(h * LANES, (h + 1) * LANES)
            oh = o_ref[0, rows, lanes]
            ms = jnp.mean(oh * oh, axis=-1, keepdims=True)
            gate = g_ref[0, rows, lanes].astype(F32)
            mg_scr[:, lanes] = (oh * lax.rsqrt(ms + NORM_EPS) * hg_ref[:, lanes] * _silu(gate)).astype(BF16)
        y = _dot(mg_scr[...], wo_ref[...])
        x2 = _post_norm(x1_ref[0, rows], y, m[5:6], 1.0, lng_ref[1:2], lnb_ref[1:2])
        return x2, (_ln(x2) * (1.0 + m[6:7]) + m[7:8]).astype(BF16)

    def hidden(h3):
        for c0 in range(0, d_ff, FF_CHUNK):
            g = _dot(h3, w2i_ref[:, c0:c0 + FF_CHUNK])
            u = _dot(h3, w2i_ref[:, d_ff + c0:d_ff + c0 + FF_CHUNK])
            a_scr[:, c0:c0 + FF_CHUNK] = (_silu(g) * u).astype(BF16)

    x2, h3 = prologue(subs[0])
    for k, rows in enumerate(subs):
        hidden(h3)
        if k + 1 < len(subs):
            x2_next, h3_next = prologue(subs[k + 1])
        y_scr[...] = _dot(a_scr[...], w2o_ref[...])
        out_ref[0, rows] = _post_norm(x2, y_scr[...], m[8:9], FFN_HALF, lng_ref[2:3], lnb_ref[2:3])
        if k + 1 < len(subs):
            x2, h3 = x2_next, h3_next


def _out_ffn_call(x1, o, g, mod, lng, lnb, hgain, wo, w2i, w2o, tm):
    bsz, t, d = x1.shape
    nt = t // tm
    d_ff = w2o.shape[0]
    tok = lambda w: pl.BlockSpec((1, tm, w), lambda b, i: (b, i, 0))
    return pl.pallas_call(
        _out_ffn_kernel,
        grid=(bsz, nt),
        in_specs=[tok(d), tok(o.shape[2]), tok(g.shape[2]),
                  pl.BlockSpec((1, N_MOD, d), lambda b, i: (b, 0, 0)),
                  _const_spec(lng.shape), _const_spec(lnb.shape), _const_spec(hgain.shape),
                  _const_spec(wo.shape), _const_spec(w2i.shape), _const_spec(w2o.shape)],
        out_specs=tok(d),
        out_shape=jax.ShapeDtypeStruct((bsz, t, d), F32),
        scratch_shapes=[pltpu.VMEM((SUB_TILE, d_ff), BF16), pltpu.VMEM((SUB_TILE, o.shape[2]), BF16),
                        pltpu.VMEM((SUB_TILE, d), F32)],
        compiler_params=pltpu.CompilerParams(dimension_semantics=("arbitrary", "arbitrary"),
                                             vmem_limit_bytes=VMEM_LIMIT),
        name="out_ffn",
    )(x1, o, g, mod, lng, lnb, hgain, wo, w2i, w2o)


def _sincos_2d(rows, width, dim):
    r = jnp.repeat(jnp.arange(rows), width)
    col = jnp.tile(jnp.arange(width), rows)
    quarter = dim // 4
    omega = 1.0 / POS_THETA ** (jnp.arange(quarter, dtype=F32) / quarter)

    def emb(p):
        a = p.astype(F32)[:, None] * omega[None, :]
        return jnp.concatenate([jnp.sin(a), jnp.cos(a)], axis=-1)

    return jnp.concatenate([emb(r), emb(col)], axis=-1)


def _mix_weights(w_mix_in, a2_f, a2_b, ab_f, ab_b):
    d = w_mix_in.shape[0]
    o = np.cumsum((0, GLA_QK, GLA_QK, GLA_V, GLA_V, GLA_GATE_RANK, GLA_GATE_RANK, HGRN_K, HGRN_K, HGRN_K, HGRN_V, HGRN_V))
    seg = lambda i: w_mix_in[:, o[i]:o[i + 1]]
    pad = jnp.zeros((d, LANES - 2 * GLA_GATE_RANK), w_mix_in.dtype)
    wmix = jnp.concatenate([seg(0), seg(1), seg(2), seg(3), seg(6), seg(7), seg(8), seg(9), seg(10), seg(4), seg(5), pad],
                           axis=1).astype(BF16)
    a2 = jnp.zeros((LANES, 2 * GLA_QK), F32)
    a2 = a2.at[0:GLA_GATE_RANK, 0:GLA_QK].set(a2_f).at[GLA_GATE_RANK:2 * GLA_GATE_RANK, GLA_QK:].set(a2_b)
    ab = jnp.concatenate([ab_f, ab_b])[None, :]
    return wmix, a2.astype(BF16), ab


def kernel(x, c, ctx, c_ctx, w_ada, b_ada, ln_gain, ln_bias, ffn1_w_in, ffn1_w_out, w_mix_in, gla_a2_fwd, gla_a2_bwd,
           gla_a_bias_fwd, gla_a_bias_bwd, hgrn_lb_logits, gla_norm_gain, hgrn_norm_gain, w_mix_out, ffn2_w_in,
           ffn2_w_out):
    bsz, t, d = x.shape
    tm = min(256, t)
    tmc = min(256, ctx.shape[1])
    pos = _sincos_2d(t // GRID_W, GRID_W, d).astype(x.dtype)

    mod_rows = -(-(bsz + 1) // 8) * 8
    cc = jnp.zeros((mod_rows, d), F32).at[:bsz].set(c).at[bsz].set(c_ctx)
    mod = _ada_call(cc, w_ada[0], b_ada[0][None, :]).reshape(mod_rows, N_MOD, d)

    w1i, w1o = ffn1_w_in[0].astype(BF16), ffn1_w_out[0].astype(BF16)
    w2i, w2o = ffn2_w_in[0].astype(BF16), ffn2_w_out[0].astype(BF16)
    wmix, a2, ab = _mix_weights(w_mix_in[0], gla_a2_fwd[0], gla_a2_bwd[0], gla_a_bias_fwd[0], gla_a_bias_bwd[0])
    lng, lnb = ln_gain[0], ln_bias[0]
    hgain = jnp.concatenate([jnp.tile(gla_norm_gain[0], GLA_HEADS), jnp.tile(hgrn_norm_gain[0], HGRN_HEADS)])[None, :]

    ktf_c, ktb_c, v_c, dlf_c, dlb_c = _ffn_proj_call(False, ctx, None, mod, bsz, lng, lnb, w1i, w1o, wmix, a2, ab,
                                                     hgrn_lb_logits, None, tmc)
    sf0, sb0 = _ctx_state_call(ktf_c, ktb_c, v_c, dlf_c, dlb_c)
    x1, ff, fb, v, g, dlf, dlb, sb = _ffn_proj_call(True, x, pos, mod, bsz, lng, lnb, w1i, w1o, wmix, a2, ab,
                                                    hgrn_lb_logits, sb0, tm)
    o = _scan_call(ff, fb, v, dlf, dlb, sb, sf0, min(4 * SUB_TILE, t))
    return _out_ffn_call(x1, o, g, mod, lng, lnb, hgain, w_mix_out[0].astype(BF16), w2i, w2o, min(2 * SUB_TILE, t))
```

```python
import functools

import numpy as np
import jax
import jax.numpy as jnp
from jax import lax
from jax.experimental import pallas as pl
from jax.experimental.pallas import tpu as pltpu

N_SUBLAYERS = 3
N_MOD = 3 * N_SUBLAYERS
FFN_HALF = 0.5
GLA_HEADS = 4
GLA_DK = 64
GLA_DV = 128
GLA_GATE_RANK = 16
GLA_GATE_NORMALIZER = 16.0
HGRN_HEADS = 4
HGRN_DK = 128
HGRN_DV = 128
GLA_QK = GLA_HEADS * GLA_DK
GLA_V = GLA_HEADS * GLA_DV
HGRN_K = HGRN_HEADS * HGRN_DK
HGRN_V = HGRN_HEADS * HGRN_DV
CHUNK = 64
HALF = CHUNK // 2
CHUNK_SHIFT = 6
LN_EPS = 1e-5
NORM_EPS = 1e-6
POS_THETA = 10000.0
GRID_W = 64
DEPTH = 1
DN_ALPHA = (2.0 * DEPTH) ** 0.25

LANES = 128
SUBLANES = 8
FF_CHUNK = 256
SUB_TILE = 256
EARLY_HIDDEN = 3
ROW_PART = 128
DL_GROUPS = 4
DECAY_W = GLA_QK + HGRN_K
N_GROUPS = GLA_HEADS // 2 + HGRN_HEADS
N_HEADS = GLA_HEADS + HGRN_HEADS
VMEM_LIMIT = 56 * 1024 * 1024

_C_GQ, _C_GK, _C_GV, _C_GG = 0, 256, 512, 1024
_C_RQ, _C_RFF, _C_RFB, _C_RI, _C_RG, _C_LR = 1536, 2048, 2560, 3072, 3584, 4096
MIX_W = 4096 + LANES

F32 = jnp.float32
BF16 = jnp.bfloat16


def _dot(a, b):
    return jnp.dot(a, b, preferred_element_type=F32)


def _dot_nt(a, b):
    return lax.dot_general(a, b, (((1,), (1,)), ((), ())), preferred_element_type=F32)


def _dot_tn(a, b):
    return lax.dot_general(a, b, (((0,), (0,)), ((), ())), preferred_element_type=F32)


def _sigmoid(x):
    return 1.0 / (1.0 + jnp.exp(-x))


def _silu(x):
    return x * _sigmoid(x)


def _ln(x):
    mu = jnp.mean(x, axis=-1, keepdims=True)
    xc = x - mu
    var = jnp.mean(xc * xc, axis=-1, keepdims=True)
    return xc * lax.rsqrt(var + LN_EPS)


def _post_norm(x, y, gate, weight, gain, bias):
    return _ln(DN_ALPHA * x + (weight * gate) * y) * gain + bias


def _ada_kernel(c_ref, w_ref, b_ref, o_ref):
    s = _silu(c_ref[...]).astype(BF16)
    o_ref[...] = _dot(s, w_ref[...].astype(BF16)) + b_ref[...]


def _ada_call(cc, w, b):
    rows, d = cc.shape
    n = w.shape[1]
    tn = 1536
    return pl.pallas_call(
        _ada_kernel,
        grid=(n // tn,),
        in_specs=[pl.BlockSpec((rows, d), lambda j: (0, 0)),
                  pl.BlockSpec((d, tn), lambda j: (0, j)),
                  pl.BlockSpec((1, tn), lambda j: (0, j))],
        out_specs=pl.BlockSpec((rows, tn), lambda j: (0, j)),
        out_shape=jax.ShapeDtypeStruct((rows, n), F32),
        compiler_params=pltpu.CompilerParams(dimension_semantics=("arbitrary",), vmem_limit_bytes=VMEM_LIMIT),
        name="ada",
    )(cc, w, b)


def _lane_masks():
    lane = lax.broadcasted_iota(jnp.int32, (1, LANES), 1)
    return lane < GLA_DK, lane >= GLA_DK


def _stack_pair(a, m0, m1):
    zero = jnp.zeros_like(a)
    return jnp.concatenate([jnp.where(m0, a, zero), jnp.where(m1, a, zero)], axis=0)


def _state_update(st_ref, dl, grp, kt, v):
    m0, m1 = _lane_masks()
    ktg = kt[:, grp * LANES:(grp + 1) * LANES]
    dlg = dl[:, grp * LANES:(grp + 1) * LANES]
    if grp < GLA_HEADS // 2:
        h0 = 2 * grp
        vs = jnp.concatenate([v[:, h0 * GLA_DV:(h0 + 1) * GLA_DV], v[:, (h0 + 1) * GLA_DV:(h0 + 2) * GLA_DV]], axis=0)
        ks = _stack_pair(ktg, m0, m1)
    else:
        h = grp - GLA_HEADS // 2
        vs = v[:, GLA_V + h * HGRN_DV:GLA_V + (h + 1) * HGRN_DV]
        ks = ktg
    st_ref[grp] = dlg * st_ref[grp] + _dot_tn(vs, ks)


def _seg_cumsum(x, reverse):
    row = lax.broadcasted_iota(jnp.int32, (SUBLANES, 1), 0)
    groups = [x[r:r + SUBLANES] for r in range(0, CHUNK, SUBLANES)]
    out = []
    carry = None
    for g in (reversed(groups) if reverse else groups):
        s = 1
        while s < SUBLANES:
            if reverse:
                g = g + jnp.where(row < SUBLANES - s, pltpu.roll(g, SUBLANES - s, 0), 0.0)
            else:
                g = g + jnp.where(row >= s, pltpu.roll(g, s, 0), 0.0)
            s *= 2
        if carry is not None:
            g = g + carry
        carry = g[0:1] if reverse else g[SUBLANES - 1:SUBLANES]
        out.append(g)
    return jnp.concatenate(out[::-1] if reverse else out, axis=0)


def _ffn_proj_kernel(latent, *refs):
    if latent:
        (x_ref, pos_ref, mod_ref, lng_ref, lnb_ref, w1i_ref, w1o_ref, wmix_ref, a2_ref, ab_ref, lbl_ref, sb0_ref,
         x1_ref, ff_ref, fb_ref, v_ref, g_ref, dlf_ref, dlb_ref, sbo_ref, a_scr, kb_scr, sb_scr) = refs
    else:
        (x_ref, mod_ref, lng_ref, lnb_ref, w1i_ref, w1o_ref, wmix_ref, a2_ref, ab_ref, lbl_ref,
         ff_ref, fb_ref, v_ref, dlf_ref, dlb_ref, a_scr) = refs
    tm = x_ref.shape[1]
    cpt = tm // CHUNK
    d_ff = w1o_ref.shape[0]
    m = mod_ref[0]
    x = x_ref[0]
    if latent:
        x = x + pos_ref[...]
    rp = min(ROW_PART, tm)
    parts = [slice(r, r + rp) for r in range(0, tm, rp)]

    def hidden(hh, rows, c0):
        g = _dot(hh, w1i_ref[:, c0:c0 + FF_CHUNK])
        u = _dot(hh, w1i_ref[:, d_ff + c0:d_ff + c0 + FF_CHUNK])
        a_scr[rows, c0:c0 + FF_CHUNK] = (_silu(g) * u).astype(BF16)

    hs = []
    for rows in parts:
        hp = (_ln(x[rows]) * (1.0 + m[0:1]) + m[1:2]).astype(BF16)
        hs.append(hp)
        for c0 in range(0, EARLY_HIDDEN * FF_CHUNK, FF_CHUNK):
            hidden(hp, rows, c0)
    h = jnp.concatenate(hs, axis=0)
    for c0 in range(EARLY_HIDDEN * FF_CHUNK, d_ff, FF_CHUNK):
        hidden(h, slice(None), c0)

    ys = [_dot(a_scr[rows, :], w1o_ref[...]) for rows in parts]

    staged = []
    for rows, y in zip(parts, ys):
        x1 = _post_norm(x[rows], y, m[2:3], FFN_HALF, lng_ref[0:1], lnb_ref[0:1])
        if latent:
            x1_ref[0, rows] = x1
        h2 = (_ln(x1) * (1.0 + m[3:4]) + m[4:5]).astype(BF16)

        def proj(c0, width, h2=h2):
            return _dot(h2, wmix_ref[:, c0:c0 + width])

        st = {}
        lr = proj(_C_LR, LANES).astype(BF16)
        st["z"] = _dot(lr, a2_ref[...]) + ab_ref[...]
        st["k"] = proj(_C_GK, GLA_QK)
        st["rf"] = (proj(_C_RFF, HGRN_K), proj(_C_RFB, HGRN_K))
        v_ref[0, rows, 0:GLA_V] = proj(_C_GV, GLA_V).astype(BF16)
        v_ref[0, rows, GLA_V:GLA_V + HGRN_V] = proj(_C_RI, HGRN_V).astype(BF16)
        if latent:
            st["q"] = proj(_C_GQ, GLA_QK) * (GLA_DK ** -0.5)
            st["rq"] = proj(_C_RQ, HGRN_K)
            g_ref[0, rows, 0:GLA_V] = proj(_C_GG, GLA_V).astype(BF16)
            g_ref[0, rows, GLA_V:GLA_V + HGRN_V] = proj(_C_RG, HGRN_V).astype(BF16)
        staged.append(st)

    lbl = lbl_ref[...]
    run = None
    for part, st in zip(parts, staged):
        z = st["z"]
        g_gla = (jnp.minimum(z, 0.0) - jnp.log(1.0 + jnp.exp(-jnp.abs(z)))) * (1.0 / GLA_GATE_NORMALIZER)
        if latent:
            q_h = _silu(st["rq"]) * (HGRN_DK ** -0.5)
        for dirn, (f_ref, dl_ref) in enumerate(((ff_ref, dlf_ref), (fb_ref, dlb_ref))):
            levels = [lbl[dirn, i:i + 1, :] for i in range(lbl.shape[1])]
            mx = functools.reduce(jnp.maximum, levels)
            es = [jnp.exp(r - mx) for r in levels]
            lb = es[0] / functools.reduce(lambda a, b: a + b, es)
            f = lb + (1.0 - lb) * _sigmoid(st["rf"][dirn])
            logf = jnp.log(f)
            kk = 1.0 - f
            g_d = g_gla[:, dirn * GLA_QK:(dirn + 1) * GLA_QK]
            for cl in range(rp // CHUNK):
                c = part.start // CHUNK + cl
                l0, l1 = cl * CHUNK, (cl + 1) * CHUNK
                r0, r1 = c * CHUNK, (c + 1) * CHUNK
                b = _seg_cumsum(jnp.concatenate([g_d[l0:l1], logf[l0:l1]], axis=1), reverse=(dirn == 1))
                bl = b[0:1] if dirn == 1 else b[CHUNK - 1:CHUNK]
                bm = b[HALF:HALF + 1] if dirn == 1 else b[HALF - 1:HALF]
                kcat = jnp.concatenate([st["k"][l0:l1], kk[l0:l1]], axis=1)
                dl = jnp.exp(bl)
                if latent:
                    qcat = jnp.concatenate([st["q"][l0:l1], q_h[l0:l1]], axis=1)
                    f_ref[0, r0:r1, 0:DECAY_W] = (qcat * jnp.exp(b - bm)).astype(BF16)
                    f_ref[0, r0:r1, DECAY_W:2 * DECAY_W] = (kcat * jnp.exp(bm - b)).astype(BF16)
                    if dirn == 1:
                        ktail = kcat * jnp.exp(bl - b)
                        kb_scr[r0:r1, :] = (ktail if run is None else ktail * run).astype(BF16)
                        run = dl if run is None else run * dl
                else:
                    f_ref[0, r0:r1, :] = (kcat * jnp.exp(bl - b)).astype(BF16)
                dl_ref[0, c:c + 1, :] = dl
                dl_ref[0, cpt + c:cpt + c + 1, :] = jnp.exp(bm)
                dl_ref[0, 2 * cpt + c:2 * cpt + c + 1, :] = jnp.exp(bl - bm)
                dl_ref[0, 3 * cpt + c:3 * cpt + c + 1, :] = jnp.zeros_like(dl)

    if latent:
        @pl.when(pl.program_id(1) == 0)
        def _():
            sb_scr[...] = sb0_ref[0]

        sbo_ref[0, 0] = sb_scr[...]
        kb = kb_scr[...]
        vv = v_ref[0]
        for grp in range(N_GROUPS):
            _state_update(sb_scr, run, grp, kb, vv)


def _const_spec(shape):
    nd = len(shape)
    return pl.BlockSpec(shape, lambda *_: (0,) * nd, pipeline_mode=pl.Buffered(1))


def _ffn_proj_call(latent, x, pos, mod, mod_row0, lng, lnb, w1i, w1o, wmix, a2, ab, lbl, sb0, tm):
    bsz, t, d = x.shape
    nt = t // tm
    d_ff = w1o.shape[0]
    cpt = tm // CHUNK
    tile = (lambda b, i: (b, nt - 1 - i, 0)) if latent else (lambda b, i: (b, i, 0))
    tok = lambda w: pl.BlockSpec((1, tm, w), tile)
    in_specs = [tok(d)]
    args = [x]
    if latent:
        in_specs.append(pl.BlockSpec((tm, d), lambda b, i: (nt - 1 - i, 0)))
        args.append(pos)
        mod_map = lambda b, i: (b, 0, 0)
    else:
        mod_map = lambda b, i: (mod_row0, 0, 0)
    in_specs += [pl.BlockSpec((1, N_MOD, d), mod_map),
                 _const_spec(lng.shape), _const_spec(lnb.shape), _const_spec(w1i.shape), _const_spec(w1o.shape),
                 _const_spec(wmix.shape), _const_spec(a2.shape), _const_spec(ab.shape), _const_spec(lbl.shape)]
    args += [mod, lng, lnb, w1i, w1o, wmix, a2, ab, lbl]
    fw = 2 * DECAY_W if latent else DECAY_W
    dl_spec = pl.BlockSpec((1, DL_GROUPS * cpt, DECAY_W), tile)
    dl_shape = jax.ShapeDtypeStruct((bsz, DL_GROUPS * t // CHUNK, DECAY_W), F32)
    f_shape = jax.ShapeDtypeStruct((bsz, t, fw), BF16)
    v_shape = jax.ShapeDtypeStruct((bsz, t, GLA_V + HGRN_V), BF16)
    scratch = [pltpu.VMEM((tm, d_ff), BF16)]
    if latent:
        s_block = (1, N_GROUPS, LANES, LANES)
        in_specs.append(pl.BlockSpec(s_block, lambda b, i: (b, 0, 0, 0)))
        args.append(sb0)
        out_specs = [tok(d), tok(fw), tok(fw), tok(GLA_V + HGRN_V), tok(GLA_V + HGRN_V), dl_spec, dl_spec,
                     pl.BlockSpec((1,) + s_block, lambda b, i: (b, nt - 1 - i, 0, 0, 0))]
        out_shape = [jax.ShapeDtypeStruct((bsz, t, d), F32), f_shape, f_shape, v_shape, v_shape, dl_shape, dl_shape,
                     jax.ShapeDtypeStruct((bsz, nt) + s_block[1:], F32)]
        scratch += [pltpu.VMEM((tm, DECAY_W), BF16), pltpu.VMEM(s_block[1:], F32)]
    else:
        out_specs = [tok(fw), tok(fw), tok(GLA_V + HGRN_V), dl_spec, dl_spec]
        out_shape = [f_shape, f_shape, v_shape, dl_shape, dl_shape]
    return pl.pallas_call(
        functools.partial(_ffn_proj_kernel, latent),
        grid=(bsz, nt),
        in_specs=in_specs,
        out_specs=out_specs,
        out_shape=out_shape,
        scratch_shapes=scratch,
        compiler_params=pltpu.CompilerParams(dimension_semantics=("arbitrary", "arbitrary"),
                                             vmem_limit_bytes=VMEM_LIMIT),
        name="ffn_proj_latent" if latent else "ffn_proj_ctx",
    )(*args)


def _ctx_state_kernel(ktf_ref, ktb_ref, v_ref, dlf_ref, dlb_ref, sf_ref, sb_ref):
    n_chunks = v_ref.shape[1] // CHUNK
    sf_ref[...] = jnp.zeros_like(sf_ref)
    sb_ref[...] = jnp.zeros_like(sb_ref)
    for i in range(n_chunks):
        for kt_ref, dl_ref, s_ref, c in ((ktf_ref, dlf_ref, sf_ref, i), (ktb_ref, dlb_ref, sb_ref, n_chunks - 1 - i)):
            kt = kt_ref[0, c * CHUNK:(c + 1) * CHUNK, :]
            v = v_ref[0, c * CHUNK:(c + 1) * CHUNK, :]
            dl = dl_ref[0, c:c + 1, :]
            for grp in range(N_GROUPS):
                _state_update(s_ref.at[0], dl, grp, kt, v)


def _ctx_state_call(ktf, ktb, v, dlf, dlb):
    bsz, tc, _ = v.shape
    full = lambda a: pl.BlockSpec((1,) + a.shape[1:], lambda b: (b,) + (0,) * (a.ndim - 1))
    s_shape = jax.ShapeDtypeStruct((bsz, N_GROUPS, LANES, LANES), F32)
    s_spec = pl.BlockSpec((1, N_GROUPS, LANES, LANES), lambda b: (b, 0, 0, 0))
    return pl.pallas_call(
        _ctx_state_kernel,
        grid=(bsz,),
        in_specs=[full(ktf), full(ktb), full(v), full(dlf), full(dlb)],
        out_specs=[s_spec, s_spec],
        out_shape=[s_shape, s_shape],
        compiler_params=pltpu.CompilerParams(dimension_semantics=("arbitrary",), vmem_limit_bytes=VMEM_LIMIT),
        name="ctx_state",
    )(ktf, ktb, v, dlf, dlb)


def _scan_kernel(ff_all, fb_all, v_all, dlf_all, dlb_all, sb_all, sf0_ref, o_all, sf_scr):
    tb = SUB_TILE
    nc = tb // CHUNK

    @pl.when(pl.program_id(1) == 0)
    def _():
        for h in range(N_HEADS):
            sf_scr[h] = sf0_ref[0, h // 2 if h < GLA_HEADS else h - GLA_HEADS // 2]

    ii = lax.broadcasted_iota(jnp.int32, (tb, tb), 0)
    jj = lax.broadcasted_iota(jnp.int32, (tb, tb), 1)
    same = (ii >> CHUNK_SHIFT) == (jj >> CHUNK_SHIFT)
    mask_f = same & (jj <= ii)
    mask_b = same & (jj >= ii)
    tok_chunk = lax.broadcasted_iota(jnp.int32, (1, tb), 1) >> CHUNK_SHIFT
    m_pair = _lane_masks()

    n_sub = v_all.shape[1] // tb
    for sub, h in [(sub, h) for sub in range(n_sub) for h in range(N_HEADS)]:
        ff_ref, fb_ref, v_ref, o_ref = (r.at[:, sub * tb:(sub + 1) * tb] for r in (ff_all, fb_all, v_all, o_all))
        dlf_ref, dlb_ref = (r.at[:, sub * DL_GROUPS * nc:(sub + 1) * DL_GROUPS * nc] for r in (dlf_all, dlb_all))
        sb_ref = sb_all.at[:, sub:sub + 1]
        grp = h // 2 if h < GLA_HEADS else h - GLA_HEADS // 2
        lo = grp * LANES
        vcol = h * LANES
        qf = ff_ref[0, :, lo:lo + LANES]
        qb = fb_ref[0, :, lo:lo + LANES]
        if h < GLA_HEADS:
            qf = jnp.where(m_pair[h % 2], qf, jnp.zeros_like(qf))
            qb = jnp.where(m_pair[h % 2], qb, jnp.zeros_like(qb))
        kif = ff_ref[0, :, DECAY_W + lo:DECAY_W + lo + LANES]
        kib = fb_ref[0, :, DECAY_W + lo:DECAY_W + lo + LANES]
        vh = v_ref[0, :, vcol:vcol + LANES]

        vt = vh.T
        zero = jnp.zeros_like(vt)
        lhs = jnp.concatenate([jnp.where(tok_chunk == c, vt, zero) for c in range(nc)], axis=0)
        u = _dot(lhs, jnp.concatenate([kif, kib], axis=1))

        st_f = [None] * nc
        cur = sf_scr[h]
        for c in range(nc):
            st_f[c] = (dlf_ref[0, nc + c:nc + c + 1, lo:lo + LANES] * cur).astype(BF16)
            cur = (dlf_ref[0, c:c + 1, lo:lo + LANES] * cur
                   + dlf_ref[0, 2 * nc + c:2 * nc + c + 1, lo:lo + LANES] * u[c * LANES:(c + 1) * LANES, 0:LANES])
        sf_scr[h] = cur
        st_b = [None] * nc
        cur = sb_ref[0, 0, grp]
        for c in reversed(range(nc)):
            st_b[c] = (dlb_ref[0, nc + c:nc + c + 1, lo:lo + LANES] * cur).astype(BF16)
            cur = (dlb_ref[0, c:c + 1, lo:lo + LANES] * cur
                   + dlb_ref[0, 2 * nc + c:2 * nc + c + 1, lo:lo + LANES]
                   * u[c * LANES:(c + 1) * LANES, LANES:2 * LANES])

        att = jnp.where(mask_f, _dot_nt(qf, kif), 0.0) + jnp.where(mask_b, _dot_nt(qb, kib), 0.0)
        o = _dot(att.astype(BF16), vh)
        for c in range(nc):
            r0, r1 = c * CHUNK, (c + 1) * CHUNK
            qc = jnp.concatenate([qf[r0:r1], qb[r0:r1]], axis=1)
            sc = jnp.concatenate([st_f[c], st_b[c]], axis=1)
            o_ref[0, r0:r1, vcol:vcol + LANES] = o[r0:r1] + _dot_nt(qc, sc)


def _scan_call(ff, fb, v, dlf, dlb, sb, sf0, tb):
    bsz, t, _ = v.shape
    nt = t // tb
    cpt = tb // CHUNK
    tok = lambda w: pl.BlockSpec((1, tb, w), lambda b, i: (b, i, 0))
    s_block = (1, N_GROUPS, LANES, LANES)
    n_sub = tb // SUB_TILE
    return pl.pallas_call(
        _scan_kernel,
        grid=(bsz, nt),
        in_specs=[tok(ff.shape[2]), tok(fb.shape[2]), tok(v.shape[2]),
                  pl.BlockSpec((1, DL_GROUPS * cpt, DECAY_W), lambda b, i: (b, i, 0)),
                  pl.BlockSpec((1, DL_GROUPS * cpt, DECAY_W), lambda b, i: (b, i, 0)),
                  pl.BlockSpec((1, n_sub) + s_block[1:], lambda b, i: (b, i, 0, 0, 0)),
                  pl.BlockSpec(s_block, lambda b, i: (b, 0, 0, 0))],
        out_specs=tok(GLA_V + HGRN_V),
        out_shape=jax.ShapeDtypeStruct((bsz, t, GLA_V + HGRN_V), F32),
        scratch_shapes=[pltpu.VMEM((N_HEADS, LANES, LANES), F32)],
        compiler_params=pltpu.CompilerParams(dimension_semantics=("arbitrary", "arbitrary"),
                                             vmem_limit_bytes=VMEM_LIMIT),
        name="scan",
    )(ff, fb, v, dlf, dlb, sb, sf0)


def _out_ffn_kernel(x1_ref, o_ref, g_ref, mod_ref, lng_ref, lnb_ref, hg_ref, wo_ref, w2i_ref, w2o_ref,
                    out_ref, a_scr, mg_scr, y_scr):
    m = mod_ref[0]
    d_ff = w2o_ref.shape[0]
    subs = [slice(r0, r0 + SUB_TILE) for r0 in range(0, x1_ref.shape[1], SUB_TILE)]

    def prologue(rows):
        for h in range(N_HEADS):
            lanes = slice(h * LANES, (h + 1) * LANES)
            oh = o_ref[0, rows, lanes]
            ms = jnp.mean(oh * oh, axis=-1, keepdims=True)
            gate = g_ref[0, rows, lanes].astype(F32)
            mg_scr[:, lanes] = (oh * lax.rsqrt(ms + NORM_EPS) * hg_ref[:, lanes] * _silu(gate)).astype(BF16)
        y = _dot(mg_scr[...], wo_ref[...])
        x2 = _post_norm(x1_ref[0, rows], y, m[5:6], 1.0, lng_ref[1:2], lnb_ref[1:2])
        return x2, (_ln(x2) * (1.0 + m[6:7]) + m[7:8]).astype(BF16)

    def hidden(h3):
        for c0 in range(0, d_ff, FF_CHUNK):
            g = _dot(h3, w2i_ref[:, c0:c0 + FF_CHUNK])
            u = _dot(h3, w2i_ref[:, d_ff + c0:d_ff + c0 + FF_CHUNK])
            a_scr[:, c0:c0 + FF_CHUNK] = (_silu(g) * u).astype(BF16)

    x2, h3 = prologue(subs[0])
    for k, rows in enumerate(subs):
        hidden(h3)
        if k + 1 < len(subs):
            x2_next, h3_next = prologue(subs[k + 1])
        y_scr[...] = _dot(a_scr[...], w2o_ref[...])
        out_ref[0, rows] = _post_norm(x2, y_scr[...], m[8:9], FFN_HALF, lng_ref[2:3], lnb_ref[2:3])
        if k + 1 < len(subs):
            x2, h3 = x2_next, h3_next


def _out_ffn_call(x1, o, g, mod, lng, lnb, hgain, wo, w2i, w2o, tm):
    bsz, t, d = x1.shape
    nt = t // tm
    d_ff = w2o.shape[0]
    tok = lambda w: pl.BlockSpec((1, tm, w), lambda b, i: (b, i, 0))
    return pl.pallas_call(
        _out_ffn_kernel,
        grid=(bsz, nt),
        in_specs=[tok(d), tok(o.shape[2]), tok(g.shape[2]),
                  pl.BlockSpec((1, N_MOD, d), lambda b, i: (b, 0, 0)),
                  _const_spec(lng.shape), _const_spec(lnb.shape), _const_spec(hgain.shape),
                  _const_spec(wo.shape), _const_spec(w2i.shape), _const_spec(w2o.shape)],
        out_specs=tok(d),
        out_shape=jax.ShapeDtypeStruct((bsz, t, d), F32),
        scratch_shapes=[pltpu.VMEM((SUB_TILE, d_ff), BF16), pltpu.VMEM((SUB_TILE, o.shape[2]), BF16),
                        pltpu.VMEM((SUB_TILE, d), F32)],
        compiler_params=pltpu.CompilerParams(dimension_semantics=("arbitrary", "arbitrary"),
                                             vmem_limit_bytes=VMEM_LIMIT),
        name="out_ffn",
    )(x1, o, g, mod, lng, lnb, hgain, wo, w2i, w2o)


def _sincos_2d(rows, width, dim):
    r = jnp.repeat(jnp.arange(rows), width)
    col = jnp.tile(jnp.arange(width), rows)
    quarter = dim // 4
    omega = 1.0 / POS_THETA ** (jnp.arange(quarter, dtype=F32) / quarter)

    def emb(p):
        a = p.astype(F32)[:, None] * omega[None, :]
        return jnp.concatenate([jnp.sin(a), jnp.cos(a)], axis=-1)

    return jnp.concatenate([emb(r), emb(col)], axis=-1)


def _mix_weights(w_mix_in, a2_f, a2_b, ab_f, ab_b):
    d = w_mix_in.shape[0]
    o = np.cumsum((0, GLA_QK, GLA_QK, GLA_V, GLA_V, GLA_GATE_RANK, GLA_GATE_RANK, HGRN_K, HGRN_K, HGRN_K, HGRN_V, HGRN_V))
    seg = lambda i: w_mix_in[:, o[i]:o[i + 1]]
    pad = jnp.zeros((d, LANES - 2 * GLA_GATE_RANK), w_mix_in.dtype)
    wmix = jnp.concatenate([seg(0), seg(1), seg(2), seg(3), seg(6), seg(7), seg(8), seg(9), seg(10), seg(4), seg(5), pad],
                           axis=1).astype(BF16)
    a2 = jnp.zeros((LANES, 2 * GLA_QK), F32)
    a2 = a2.at[0:GLA_GATE_RANK, 0:GLA_QK].set(a2_f).at[GLA_GATE_RANK:2 * GLA_GATE_RANK, GLA_QK:].set(a2_b)
    ab = jnp.concatenate([ab_f, ab_b])[None, :]
    return wmix, a2.astype(BF16), ab


def kernel(x, c, ctx, c_ctx, w_ada, b_ada, ln_gain, ln_bias, ffn1_w_in, ffn1_w_out, w_mix_in, gla_a2_fwd, gla_a2_bwd,
           gla_a_bias_fwd, gla_a_bias_bwd, hgrn_lb_logits, gla_norm_gain, hgrn_norm_gain, w_mix_out, ffn2_w_in,
           ffn2_w_out):
    bsz, t, d = x.shape
    tm = min(256, t)
    tmc = min(256, ctx.shape[1])
    pos = _sincos_2d(t // GRID_W, GRID_W, d).astype(x.dtype)

    mod_rows = -(-(bsz + 1) // 8) * 8
    cc = jnp.zeros((mod_rows, d), F32).at[:bsz].set(c).at[bsz].set(c_ctx)
    mod = _ada_call(cc, w_ada[0], b_ada[0][None, :]).reshape(mod_rows, N_MOD, d)

    w1i, w1o = ffn1_w_in[0].astype(BF16), ffn1_w_out[0].astype(BF16)
    w2i, w2o = ffn2_w_in[0].astype(BF16), ffn2_w_out[0].astype(BF16)
    wmix, a2, ab = _mix_weights(w_mix_in[0], gla_a2_fwd[0], gla_a2_bwd[0], gla_a_bias_fwd[0], gla_a_bias_bwd[0])
    lng, lnb = ln_gain[0], ln_bias[0]
    hgain = jnp.concatenate([jnp.tile(gla_norm_gain[0], GLA_HEADS), jnp.tile(hgrn_norm_gain[0], HGRN_HEADS)])[None, :]

    ktf_c, ktb_c, v_c, dlf_c, dlb_c = _ffn_proj_call(False, ctx, None, mod, bsz, lng, lnb, w1i, w1o, wmix, a2, ab,
                                                     hgrn_lb_logits, None, tmc)
    sf0, sb0 = _ctx_state_call(ktf_c, ktb_c, v_c, dlf_c, dlb_c)
    x1, ff, fb, v, g, dlf, dlb, sb = _ffn_proj_call(True, x, pos, mod, bsz, lng, lnb, w1i, w1o, wmix, a2, ab,
                                                    hgrn_lb_logits, sb0, tm)
    o = _scan_call(ff, fb, v, dlf, dlb, sb, sf0, min(4 * SUB_TILE, t))
    return _out_ffn_call(x1, o, g, mod, lng, lnb, hgain, w_mix_out[0].astype(BF16), w2i, w2o, min(4 * SUB_TILE, t))
```
